```python
import jax, jax.numpy as jnp
from jax import lax
import numpy as np

D_MODEL = 1024
BATCH = 32
SEQ = 2048
DEPTH = 2

N_MIXERS = 2
N_A = (DEPTH + 1) // 2
N_B = DEPTH // 2
D_RNN = D_MODEL
LRU_BLOCKS = 16
LRU_BW = D_RNN // LRU_BLOCKS
LRU_CONV = 4
LRU_C = 8.0
RWKV_N = 64
RWKV_H = D_MODEL // RWKV_N
R_DECAY = 64
R_AAA = 64
R_GATE = 160
GN_EPS = 64e-5
D_FF = 3 * D_MODEL
FFN_CONV = 3
RMS_EPS = 1e-6

kernel_name = 'hybrid_rglru_rwkv7_convffn'


def _rmsnorm(x, g):
    xf = x.astype(jnp.float32)
    y = xf * lax.rsqrt(jnp.mean(xf * xf, axis=-1, keepdims=True) + RMS_EPS)
    return (y * g.astype(jnp.float32)).astype(x.dtype)


def _causal_dwconv(x, w, b):
    k_width, seq = w.shape[0], x.shape[1]
    xp = jnp.pad(x, ((0, 0), (k_width - 1, 0), (0, 0)))
    out = b
    for j in range(k_width):
        out = out + xp[:, j:j + seq] * w[j]
    return out


def _lru_combine(c1, c2):
    a1, b1 = c1
    a2, b2 = c2
    return a1 * a2, a2 * b1 + b2


def _rglru_block(x, norm, w_in, b_in, conv_w, conv_b, gate_w, gate_b, lam, w_out, b_out):
    bsz, seq, _ = x.shape
    h = _rmsnorm(x, norm)
    u = jnp.einsum('btd,de->bte', h, w_in) + b_in
    y_branch = jax.nn.gelu(u[..., :D_RNN], approximate=True)
    xr = _causal_dwconv(u[..., D_RNN:], conv_w, conv_b)
    xb = xr.reshape(bsz, seq, LRU_BLOCKS, LRU_BW)
    gates = jax.nn.sigmoid(jnp.einsum('btnc,gncd->gbtnd', xb, gate_w) + gate_b[:, None, None])
    r_gate = gates[0].reshape(bsz, seq, D_RNN).astype(jnp.float32)
    i_gate = gates[1].reshape(bsz, seq, D_RNN).astype(jnp.float32)
    log_a = -LRU_C * r_gate * jax.nn.softplus(-lam.astype(jnp.float32))
    a = jnp.exp(log_a)
    mult = jnp.sqrt(-jnp.expm1(2.0 * log_a))
    bterm = mult * (i_gate * xr.astype(jnp.float32))
    _, hs = lax.associative_scan(_lru_combine, (a, bterm), axis=1)
    out = hs.astype(x.dtype) * y_branch
    return jnp.einsum('bte,ed->btd', out, w_out) + b_out


def _rwkv7_scan(r, w, k, v, aa, bb):
    bsz, _, nh, n = r.shape

    def step(S, inp):
        r_t, w_t, k_t, v_t, a_t, b_t = inp
        sa = jnp.einsum('bhij,bhj->bhi', S, a_t)
        S = S * w_t[:, :, None, :] + sa[..., None] * b_t[:, :, None, :] + v_t[..., :, None] * k_t[..., None, :]
        y = jnp.einsum('bhij,bhj->bhi', S, r_t)
        return S, y

    s0 = jnp.zeros((bsz, nh, n, n), jnp.float32)
    xs = tuple(t.transpose(1, 0, 2, 3) for t in (r, w, k, v, aa, bb))
    _, ys = lax.scan(step, s0, xs)
    return ys.transpose(1, 0, 2, 3)


def _rwkv7_block(x, norm, mix, w_rkv, w0, w1, w2, a0, a1, a2, g1, g2, k_k, k_a, r_k, ln_w, ln_b, w_out):
    bsz, seq, d = x.shape
    f32 = jnp.float32
    h = _rmsnorm(x, norm)
    xx = jnp.pad(h, ((0, 0), (1, 0), (0, 0)))[:, :-1] - h
    xs_rkv = jnp.stack([h + xx * mix[0], h + xx * mix[1], h + xx * mix[2]])
    rkv = jnp.einsum('sbtd,sde->sbte', xs_rkv, w_rkv)
    r, k, v = rkv[0], rkv[1], rkv[2]
    xw = h + xx * mix[3]
    xa = h + xx * mix[4]
    xg = h + xx * mix[5]
    w = -jax.nn.softplus(-(w0 + jnp.tanh(xw @ w1) @ w2).astype(f32)) - 0.5
    decay = jnp.exp(-jnp.exp(w))
    a = jax.nn.sigmoid((a0 + (xa @ a1) @ a2).astype(f32))
    g = jax.nn.sigmoid(xg @ g1) @ g2
    kf = k.astype(f32)
    kk = (kf * k_k).reshape(bsz, seq, RWKV_H, RWKV_N)
    kk = kk / jnp.maximum(jnp.linalg.norm(kk, axis=-1, keepdims=True), 1e-12)
    kf = kf * (1.0 + (a - 1.0) * k_a)
    hs = lambda t: t.reshape(bsz, seq, RWKV_H, RWKV_N)
    rh, kh, vh = hs(r.astype(f32)), hs(kf), hs(v.astype(f32))
    ah = hs(a)
    y = _rwkv7_scan(rh, hs(decay), kh, vh, -kk, kk * ah)
    mu = jnp.mean(y, axis=-1, keepdims=True)
    var = jnp.mean(jnp.square(y - mu), axis=-1, keepdims=True)
    y = ((y - mu) * lax.rsqrt(var + GN_EPS)).reshape(bsz, seq, d) * ln_w + ln_b
    bonus = jnp.sum(rh * kh * r_k, axis=-1, keepdims=True) * vh
    y = (y + bonus.reshape(bsz, seq, d)).astype(x.dtype)
    return jnp.einsum('btd,de->bte', y * g, w_out)


def _conv_ffn(x, norm, w_up, conv_w, conv_b, w_down):
    h = _rmsnorm(x, norm)
    u = jnp.einsum('btd,df->btf', h, w_up)
    gate = _causal_dwconv(u[..., :D_FF], conv_w, conv_b)
    hid = jax.nn.gelu(gate, approximate=True) * u[..., D_FF:]
    return jnp.einsum('btf,fd->btd', hid, w_down)


def setup_inputs(seed: int = 0) -> dict:
    key = jax.random.key(seed)
    ks = iter(jax.random.split(key, 48))
    f32 = jnp.float32
    nrm = lambda shape, scale: jax.random.normal(next(ks), shape, f32) * scale
    uni = lambda shape, lo, hi: jax.random.uniform(next(ks), shape, f32, lo, hi)
    d = D_MODEL
    u_a = uni((N_A, D_RNN), 0.9, 0.999)
    a_init = u_a ** (1.0 / LRU_C)
    return {
        'x': nrm((BATCH, SEQ, d), 1.0),
        'lru_norm': 1.0 + nrm((N_A, d), 0.02),
        'lru_w_in': nrm((N_A, d, 2 * D_RNN), d ** -0.5),
        'lru_b_in': nrm((N_A, 2 * D_RNN), 0.01),
        'lru_conv_w': nrm((N_A, LRU_CONV, D_RNN), LRU_CONV ** -0.5),
        'lru_conv_b': nrm((N_A, D_RNN), 0.01),
        'lru_gate_w': nrm((N_A, 2, LRU_BLOCKS, LRU_BW, LRU_BW), LRU_BW ** -0.5),
        'lru_gate_b': nrm((N_A, 2, LRU_BLOCKS, LRU_BW), 0.01),
        'lru_lambda': jnp.log(a_init) - jnp.log1p(-a_init),
        'lru_w_out': nrm((N_A, D_RNN, d), D_RNN ** -0.5),
        'lru_b_out': nrm((N_A, d), 0.01),
        'rwkv_norm': 1.0 + nrm((N_B, d), 0.02),
        'rwkv_mix': uni((N_B, 6, d), 0.0, 1.0),
        'rwkv_w_rkv': nrm((N_B, 3, d, d), d ** -0.5),
        'rwkv_w0': uni((N_B, d), -5.0, -1.0),
        'rwkv_w1': nrm((N_B, d, R_DECAY), d ** -0.5),
        'rwkv_w2': nrm((N_B, R_DECAY, d), 0.1 * R_DECAY ** -0.5),
        'rwkv_a0': nrm((N_B, d), 0.1),
        'rwkv_a1': nrm((N_B, d, R_AAA), d ** -0.5),
        'rwkv_a2': nrm((N_B, R_AAA, d), 0.1 * R_AAA ** -0.5),
        'rwkv_g1': nrm((N_B, d, R_GATE), d ** -0.5),
        'rwkv_g2': nrm((N_B, R_GATE, d), R_GATE ** -0.5),
        'rwkv_k_k': 0.85 + nrm((N_B, d), 0.05),
        'rwkv_k_a': 1.0 + nrm((N_B, d), 0.05),
        'rwkv_r_k': nrm((N_B, RWKV_H, RWKV_N), 0.1),
        'rwkv_ln_w': 1.0 + nrm((N_B, d), 0.02),
        'rwkv_ln_b': nrm((N_B, d), 0.01),
        'rwkv_w_out': nrm((N_B, d, d), d ** -0.5),
        'ffn_norm': 1.0 + nrm((DEPTH, d), 0.02),
        'ffn_w_up': nrm((DEPTH, d, 2 * D_FF), d ** -0.5),
        'ffn_conv_w': nrm((DEPTH, FFN_CONV, D_FF), FFN_CONV ** -0.5),
        'ffn_conv_b': nrm((DEPTH, D_FF), 0.01),
        'ffn_w_down': nrm((DEPTH, D_FF, d), D_FF ** -0.5),
        'final_norm': 1.0 + nrm((d,), 0.02),
    }


def reference(x, lru_norm, lru_w_in, lru_b_in, lru_conv_w, lru_conv_b, lru_gate_w, lru_gate_b, lru_lambda, lru_w_out, lru_b_out,
              rwkv_norm, rwkv_mix, rwkv_w_rkv, rwkv_w0, rwkv_w1, rwkv_w2, rwkv_a0, rwkv_a1, rwkv_a2, rwkv_g1, rwkv_g2,
              rwkv_k_k, rwkv_k_a, rwkv_r_k, rwkv_ln_w, rwkv_ln_b, rwkv_w_out,
              ffn_norm, ffn_w_up, ffn_conv_w, ffn_conv_b, ffn_w_down, final_norm):
    for layer in range(DEPTH):
        j = layer // N_MIXERS
        if layer % N_MIXERS == 0:
            x = x + _rglru_block(x, lru_norm[j], lru_w_in[j], lru_b_in[j], lru_conv_w[j], lru_conv_b[j],
                                 lru_gate_w[j], lru_gate_b[j], lru_lambda[j], lru_w_out[j], lru_b_out[j])
        else:
            x = x + _rwkv7_block(x, rwkv_norm[j], rwkv_mix[j], rwkv_w_rkv[j], rwkv_w0[j], rwkv_w1[j], rwkv_w2[j],
                                 rwkv_a0[j], rwkv_a1[j], rwkv_a2[j], rwkv_g1[j], rwkv_g2[j], rwkv_k_k[j], rwkv_k_a[j],
                                 rwkv_r_k[j], rwkv_ln_w[j], rwkv_ln_b[j], rwkv_w_out[j])
        x = x + _conv_ffn(x, ffn_norm[layer], ffn_w_up[layer], ffn_conv_w[layer], ffn_conv_b[layer], ffn_w_down[layer])
    return _rmsnorm(x, final_norm)
```

```python
import functools

import jax
import jax.numpy as jnp
from jax import lax
from jax.experimental import pallas as pl
from jax.experimental.pallas import tpu as pltpu

F32 = jnp.float32
BF16 = jnp.bfloat16

RMS_EPS = 1e-6
GN_EPS = 64e-5
LRU_C = 8.0
HEAD = 64
PAIR = 2 * HEAD
GATE_TILE = 256
SCAN_L = 32
SUBLANES = 8

VMEM_LIMIT = 56 * 1024 * 1024


def _rms(x, g):
    ms = jnp.mean(x * x, axis=-1, keepdims=True)
    return x * lax.rsqrt(ms + RMS_EPS) * g


def _gelu(x):
    return x * (0.5 * (1.0 + jnp.tanh(0.7978845608028654 * (x + 0.044715 * (x * x * x)))))


def _sigmoid(x):
    return 1.0 / (1.0 + jnp.exp(-x))


def _softplus(x):
    return jnp.maximum(x, 0.0) + jnp.log1p(jnp.exp(-jnp.abs(x)))


def _dot(a, b):
    return jnp.dot(a, b, preferred_element_type=F32)


def _dot_nt(a, b):
    return lax.dot_general(a, b, (((1,), (1,)), ((), ())), preferred_element_type=F32)


def _const_spec(shape):
    nd = len(shape)
    return pl.BlockSpec(shape, lambda *_: (0,) * nd, pipeline_mode=pl.Buffered(1))


def _params():
    return pltpu.CompilerParams(dimension_semantics=("arbitrary", "arbitrary"),
                                vmem_limit_bytes=VMEM_LIMIT)


def _shifted(buf, halo, cur, tm, first):
    @pl.when(first)
    def _():
        halo[...] = jnp.zeros_like(halo)
    buf[0:SUBLANES, :] = halo[...]
    buf[SUBLANES:SUBLANES + tm, :] = cur
    halo[...] = cur[tm - SUBLANES:tm, :]


def _ffn_kernel(x_ref, g_ref, wup_ref, cw_ref, cb_ref, wdn_ref, fn_ref, o_ref,
                ubuf, halo, *, tm, fc, dff, final_norm):
    first = pl.program_id(1) == 0

    @pl.when(first)
    def _():
        halo[...] = jnp.zeros_like(halo)

    x = x_ref[...]
    h = _rms(x, g_ref[...]).astype(BF16)
    acc = jnp.zeros(x.shape, F32)
    for f in range(dff // fc):
        lo = f * fc
        ug = _dot(h, wup_ref[:, lo:lo + fc])
        uv = _dot(h, wup_ref[:, dff + lo:dff + lo + fc])
        ubuf[0:SUBLANES, :] = halo[f]
        ubuf[SUBLANES:SUBLANES + tm, :] = ug
        halo[f] = ug[tm - SUBLANES:tm, :]
        gate = (cb_ref[:, lo:lo + fc]
                + cw_ref[2:3, lo:lo + fc] * ug
                + cw_ref[1:2, lo:lo + fc] * ubuf[SUBLANES - 1:SUBLANES - 1 + tm, :]
                + cw_ref[0:1, lo:lo + fc] * ubuf[SUBLANES - 2:SUBLANES - 2 + tm, :])
        hid = (_gelu(gate) * uv).astype(BF16)
        acc = acc + _dot(hid, wdn_ref[lo:lo + fc, :])
    y = x + acc
    if final_norm:
        y = _rms(y, fn_ref[...])
    o_ref[...] = y


def _ffn(x2d, norm, w_up, conv_w, conv_b, w_down, final_g, *, nb, nt, tm, final_norm):
    m, d = x2d.shape
    dff = w_down.shape[0]
    fc = 512
    row = lambda b, c: (b * nt + c, 0)
    kern = functools.partial(_ffn_kernel, tm=tm, fc=fc, dff=dff, final_norm=final_norm)
    return pl.pallas_call(
        kern,
        grid=(nb, nt),
        in_specs=[
            pl.BlockSpec((tm, d), row),
            _const_spec((1, d)),
            _const_spec((d, 2 * dff)),
            _const_spec((3, dff)),
            _const_spec((1, dff)),
            _const_spec((dff, d)),
            _const_spec((1, d)),
        ],
        out_specs=pl.BlockSpec((tm, d), row),
        out_shape=jax.ShapeDtypeStruct((m, d), F32),
        scratch_shapes=[
            pltpu.VMEM((SUBLANES + tm, fc), F32),
            pltpu.VMEM((dff // fc, SUBLANES, fc), F32),
        ],
        compiler_params=_params(),
        name="conv_ffn",
    )(x2d, norm.reshape(1, d), w_up.astype(BF16), conv_w, conv_b.reshape(1, dff),
      w_down.astype(BF16), final_g.reshape(1, d))


def _lru_kernel(x_ref, g_ref, win_ref, bin_ref, cw_ref, cb_ref, wg_ref, bg_ref, lam_ref,
                wout_ref, bout_ref, o_ref, xbuf, halo, abuf, bbuf, hbuf, hcar, *, tm, d):
    first = pl.program_id(1) == 0

    @pl.when(first)
    def _():
        hcar[...] = jnp.zeros_like(hcar)

    x = x_ref[...]
    h = _rms(x, g_ref[...]).astype(BF16)
    ux = _dot(h, win_ref[:, d:2 * d]) + bin_ref[:, d:2 * d]
    _shifted(xbuf, halo, ux, tm, first)
    xr = (cb_ref[...]
          + cw_ref[3:4, :] * ux
          + cw_ref[2:3, :] * xbuf[SUBLANES - 1:SUBLANES - 1 + tm, :]
          + cw_ref[1:2, :] * xbuf[SUBLANES - 2:SUBLANES - 2 + tm, :]
          + cw_ref[0:1, :] * xbuf[SUBLANES - 3:SUBLANES - 3 + tm, :])

    sp = _softplus(-lam_ref[...])
    sub = lax.broadcasted_iota(jnp.int32, (tm, GATE_TILE), 0) & (SUBLANES - 1)
    for q in range(d // GATE_TILE):
        lo = q * GATE_TILE
        xq = xr[:, lo:lo + GATE_TILE]
        xqb = xq.astype(BF16)
        rg = _sigmoid(_dot(xqb, wg_ref[0, q]) + bg_ref[0:1, lo:lo + GATE_TILE])
        ig = _sigmoid(_dot(xqb, wg_ref[1, q]) + bg_ref[1:2, lo:lo + GATE_TILE])
        log_a = (-LRU_C) * rg * sp[:, lo:lo + GATE_TILE]
        a = jnp.exp(log_a)
        th = jnp.tanh(log_a)
        bt = jnp.sqrt(-2.0 * th / (1.0 - th)) * (ig * xq)
        for s in (1, 2, 4):
            keep = sub >= s
            a_sh = pltpu.roll(a, s, 0)
            b_sh = pltpu.roll(bt, s, 0)
            bt = jnp.where(keep, a * b_sh + bt, bt)
            a = jnp.where(keep, a * a_sh, a)
        abuf[:, lo:lo + GATE_TILE] = a
        bbuf[:, lo:lo + GATE_TILE] = bt

    def carry_group(i, hprev):
        r0 = pl.multiple_of(i * SUBLANES, SUBLANES)
        h8 = abuf[pl.ds(r0, SUBLANES), :] * hprev + bbuf[pl.ds(r0, SUBLANES), :]
        hbuf[pl.ds(r0, SUBLANES), :] = h8
        return h8[SUBLANES - 1:SUBLANES, :]

    hlast = lax.fori_loop(0, tm // SUBLANES, carry_group, hcar[0:1, :], unroll=8)
    hcar[0:1, :] = hlast

    uy = _dot(h, win_ref[:, 0:d]) + bin_ref[:, 0:d]
    out = (hbuf[...] * _gelu(uy)).astype(BF16)
    o_ref[...] = x + _dot(out, wout_ref[...]) + bout_ref[...]


def _lru(x2d, norm, w_in, b_in, conv_w, conv_b, gate_w, gate_b, lam, w_out, b_out, *, nb, nt, tm):
    m, d = x2d.shape
    nblk, bw = gate_w.shape[1], gate_w.shape[2]
    per = GATE_TILE // bw
    w5 = gate_w.reshape(2, nblk // per, per, bw, bw)
    eye = jnp.eye(per, dtype=gate_w.dtype)
    wg = jnp.einsum('gqmcd,mn->gqmcnd', w5, eye).reshape(2, nblk // per, GATE_TILE, GATE_TILE)
    row = lambda b, c: (b * nt + c, 0)
    kern = functools.partial(_lru_kernel, tm=tm, d=d)
    return pl.pallas_call(
        kern,
        grid=(nb, nt),
        in_specs=[
            pl.BlockSpec((tm, d), row),
            _const_spec((1, d)),
            _const_spec((d, 2 * d)),
            _const_spec((1, 2 * d)),
            _const_spec((4, d)),
            _const_spec((1, d)),
            _const_spec((2, d // GATE_TILE, GATE_TILE, GATE_TILE)),
            _const_spec((2, d)),
            _const_spec((1, d)),
            _const_spec((d, d)),
            _const_spec((1, d)),
        ],
        out_specs=pl.BlockSpec((tm, d), row),
        out_shape=jax.ShapeDtypeStruct((m, d), F32),
        scratch_shapes=[
            pltpu.VMEM((SUBLANES + tm, d), F32),
            pltpu.VMEM((SUBLANES, d), F32),
            pltpu.VMEM((tm, d), F32),
            pltpu.VMEM((tm, d), F32),
            pltpu.VMEM((tm, d), F32),
            pltpu.VMEM((SUBLANES, d), F32),
        ],
        compiler_params=_params(),
        name="rglru",
    )(x2d, norm.reshape(1, d), w_in.astype(BF16), b_in.reshape(1, 2 * d), conv_w,
      conv_b.reshape(1, d), wg.astype(BF16), gate_b.reshape(2, d), lam.reshape(1, d),
      w_out.astype(BF16), b_out.reshape(1, d))


def _head_sum(x, jmat):
    hi = x.astype(BF16)
    lo = (x - hi.astype(F32)).astype(BF16)
    return _dot(hi, jmat) + _dot(lo, jmat)


def _pre_kernel(x_ref, g_ref, mix_ref, wrkv_ref, w0_ref, w1_ref, w2_ref, a0_ref, a1_ref, a2_ref,
                g1_ref, g2_ref, kk_ref, ka_ref, rk_ref, jm_ref,
                r_out, c_out, k_out, v_out, kn_out, b_out, g_out, bonus_out,
                hbuf, halo, *, tm, d):
    first = pl.program_id(1) == 0
    x = x_ref[...]
    h = _rms(x, g_ref[...])
    _shifted(hbuf, halo, h, tm, first)
    xx = hbuf[SUBLANES - 1:SUBLANES - 1 + tm, :] - h

    def mixed(i):
        return (h + xx * mix_ref[i:i + 1, :]).astype(BF16)

    r = _dot(mixed(0), wrkv_ref[0])
    k = _dot(mixed(1), wrkv_ref[1])
    v = _dot(mixed(2), wrkv_ref[2])
    wl = _dot(jnp.tanh(_dot(mixed(3), w1_ref[...])).astype(BF16), w2_ref[...])
    w = -_softplus(-(w0_ref[...] + wl)) - 0.5
    lw = -jnp.exp(w)
    a = _sigmoid(a0_ref[...] + _dot(_dot(mixed(4), a1_ref[...]).astype(BF16), a2_ref[...]))
    g_out[...] = _dot(_sigmoid(_dot(mixed(5), g1_ref[...])).astype(BF16), g2_ref[...])

    tin = lax.broadcasted_iota(jnp.int32, (tm, d), 0) & (SCAN_L - 1)
    c = lw
    s = 1
    while s < SCAN_L:
        c = jnp.where(tin >= s, c + pltpu.roll(c, s, 0), c)
        s *= 2

    kf = k * (1.0 + (a - 1.0) * ka_ref[...])
    kk = k * kk_ref[...]
    rkk = r * kf * rk_ref[...]
    jm = jm_ref[...]
    for p in range(d // PAIR):
        sl = slice(p * PAIR, (p + 1) * PAIR)
        kkp = kk[:, sl]
        nrm = jnp.sqrt(_head_sum(kkp * kkp, jm))
        kn = kkp / jnp.maximum(nrm, 1e-12)
        kn_out[p] = kn
        b_out[p] = kn * a[:, sl]
        bonus_out[:, sl] = _head_sum(rkk[:, sl], jm) * v[:, sl]
        r_out[p] = r[:, sl]
        c_out[p] = c[:, sl]
        k_out[p] = kf[:, sl]
        v_out[p] = v[:, sl]


def _scan_kernel(r_ref, c_ref, k_ref, v_ref, kn_ref, b_ref, y_ref, s_ref, *, nu):
    L = SCAN_L

    @pl.when(pl.program_id(1) == 0)
    def _():
        s_ref[...] = jnp.zeros_like(s_ref)

    lane = lax.broadcasted_iota(jnp.int32, (L, PAIR), 1)
    trow = lax.broadcasted_iota(jnp.int32, (L, PAIR), 0)
    h0 = lane < HEAD
    col_t = lane & (L - 1)
    strict = col_t < trow
    incl = col_t <= trow
    nb0 = strict & (lane < L)
    nb1 = strict & (lane >= L) & (lane < 2 * L)
    srow = lax.broadcasted_iota(jnp.int32, (PAIR, PAIR), 0)
    scol = lax.broadcasted_iota(jnp.int32, (PAIR, PAIR), 1)
    same_head = (srow < HEAD) == (scol < HEAD)
    eye2 = (lax.broadcasted_iota(jnp.int32, (2 * L, PAIR), 0)
            == lax.broadcasted_iota(jnp.int32, (2 * L, PAIR), 1)).astype(F32)
    zeros2 = jnp.zeros((2 * L, PAIR), F32)

    def split(t):
        return jnp.concatenate([jnp.where(h0, t, 0.0), jnp.where(h0, 0.0, t)], axis=0)

    def unit(u, carry):
        r = r_ref[u]
        c = c_ref[u]
        k = k_ref[u]
        v = v_ref[u]
        kn = kn_ref[u]
        bv = b_ref[u]
        c_last = c[L - 1:L, :]
        c_excl = jnp.where(trow >= 1, pltpu.roll(c, 1, 0), 0.0)
        e_neg = jnp.exp(-c)
        e_end = jnp.exp(c_last - c)
        rt = r * jnp.exp(c)
        at = -(kn * jnp.exp(c_excl))
        lhs = jnp.concatenate([at, rt], axis=0).astype(BF16)
        rhs = jnp.concatenate([split(bv * e_neg), split(k * e_neg)], axis=0).astype(BF16)
        p = _dot_nt(lhs, rhs)
        ptop = jnp.where(strict, p[0:L], 0.0)
        pbot = jnp.where(incl, p[L:2 * L], 0.0)
        nbd = jnp.concatenate([jnp.where(nb0, ptop, 0.0), jnp.where(nb1, ptop, 0.0)], axis=0)

        tinv = eye2 + nbd
        npow = nbd
        for _ in range(4):
            npb = jnp.concatenate([npow, zeros2], axis=0).astype(BF16)
            npow = _dot(npow.astype(BF16), npb)
            tinv = tinv + _dot(tinv.astype(BF16),
                               jnp.concatenate([npow, zeros2], axis=0).astype(BF16))

        s = s_ref[u]
        a_s = _dot_nt(lhs, s.astype(BF16))
        vm = split(v)
        w = a_s[0:L] + _dot(ptop.astype(BF16), jnp.concatenate([zeros2, vm], axis=0).astype(BF16))
        x = _dot(tinv.astype(BF16), jnp.concatenate([split(w), zeros2], axis=0).astype(BF16))
        y_ref[u] = a_s[L:2 * L] + _dot(pbot.astype(BF16),
                                       jnp.concatenate([x, vm], axis=0).astype(BF16))
        uu = x[0:L] + x[L:2 * L]
        uv_t = jnp.concatenate([uu, v], axis=0).T.astype(BF16)
        bk = jnp.concatenate([bv * e_end, k * e_end], axis=0).astype(BF16)
        s_new = s * jnp.exp(c_last) + _dot(uv_t, bk)
        s_ref[u] = jnp.where(same_head, s_new, 0.0)
        return carry

    lax.fori_loop(0, nu, unit, 0)


def _post_kernel(y_ref, bonus_ref, g_ref, x_ref, lnw_ref, lnb_ref, wout_ref, jm_ref, o_ref,
                 zbuf, *, d):
    jm = jm_ref[...]
    for p in range(d // PAIR):
        sl = slice(p * PAIR, (p + 1) * PAIR)
        y = y_ref[p]
        mu = _head_sum(y, jm) * (1.0 / HEAD)
        dy = y - mu
        var = _head_sum(dy * dy, jm) * (1.0 / HEAD)
        yn = dy * lax.rsqrt(var + GN_EPS) * lnw_ref[:, sl] + lnb_ref[:, sl]
        zbuf[:, sl] = ((yn + bonus_ref[:, sl]) * g_ref[:, sl]).astype(BF16)
    o_ref[...] = x_ref[...] + _dot(zbuf[...], wout_ref[...])


def _rwkv(x2d, norm, mix, w_rkv, w0, w1, w2, a0, a1, a2, g1, g2, k_k, k_a, r_k, ln_w, ln_b, w_out,
          *, nb, nt_pre, tm_pre, nt_post, tm_post, seq):
    m, d = x2d.shape
    npair = d // PAIR
    assert 4 * SCAN_L == PAIR and tm_pre % SCAN_L == 0
    idx = jnp.arange(PAIR) // HEAD
    jm = (idx[:, None] == idx[None, :]).astype(BF16)
    vec = lambda t: t.reshape(1, d)
    row_pre = lambda b, c: (b * nt_pre + c, 0)
    pair_pre = lambda b, c: (b, c, 0)
    pair_shape = jax.ShapeDtypeStruct((nb * npair, seq, PAIR), F32)
    pair_spec = pl.BlockSpec((npair, tm_pre, PAIR), pair_pre)
    flat_shape = jax.ShapeDtypeStruct((m, d), F32)
    r, c, k, v, kn, bv, g, bonus = pl.pallas_call(
        functools.partial(_pre_kernel, tm=tm_pre, d=d),
        grid=(nb, nt_pre),
        in_specs=[
            pl.BlockSpec((tm_pre, d), row_pre),
            _const_spec((1, d)),
            _const_spec((6, d)),
            _const_spec((3, d, d)),
            _const_spec((1, d)),
            _const_spec(w1.shape),
            _const_spec(w2.shape),
            _const_spec((1, d)),
            _const_spec(a1.shape),
            _const_spec(a2.shape),
            _const_spec(g1.shape),
            _const_spec(g2.shape),
            _const_spec((1, d)),
            _const_spec((1, d)),
            _const_spec((1, d)),
            _const_spec((PAIR, PAIR)),
        ],
        out_specs=[pair_spec] * 6 + [pl.BlockSpec((tm_pre, d), row_pre)] * 2,
        out_shape=[pair_shape] * 6 + [flat_shape] * 2,
        scratch_shapes=[
            pltpu.VMEM((SUBLANES + tm_pre, d), F32),
            pltpu.VMEM((SUBLANES, d), F32),
        ],
        compiler_params=_params(),
        name="rwkv_pre",
    )(x2d, vec(norm), mix, w_rkv.astype(BF16), vec(w0), w1.astype(BF16), w2.astype(BF16),
      vec(a0), a1.astype(BF16), a2.astype(BF16), g1.astype(BF16), g2.astype(BF16),
      vec(k_k), vec(k_a), vec(r_k), jm)

    nu = 4 * npair
    unit_spec = pl.BlockSpec((nu, SCAN_L, PAIR), lambda ug, c: (ug, c, 0))
    y = pl.pallas_call(
        functools.partial(_scan_kernel, nu=nu),
        grid=(nb * npair // nu, seq // SCAN_L),
        in_specs=[unit_spec] * 6,
        out_specs=unit_spec,
        out_shape=pair_shape,
        scratch_shapes=[pltpu.VMEM((nu, PAIR, PAIR), F32)],
        compiler_params=_params(),
        name="rwkv_scan",
    )(r, c, k, v, kn, bv)

    row_post = lambda b, c: (b * nt_post + c, 0)
    return pl.pallas_call(
        functools.partial(_post_kernel, d=d),
        grid=(nb, nt_post),
        in_specs=[
            pl.BlockSpec((npair, tm_post, PAIR), lambda b, c: (b, c, 0)),
            pl.BlockSpec((tm_post, d), row_post),
            pl.BlockSpec((tm_post, d), row_post),
            pl.BlockSpec((tm_post, d), row_post),
            _const_spec((1, d)),
            _const_spec((1, d)),
            _const_spec((d, d)),
            _const_spec((PAIR, PAIR)),
        ],
        out_specs=pl.BlockSpec((tm_post, d), row_post),
        out_shape=flat_shape,
        scratch_shapes=[pltpu.VMEM((tm_post, d), BF16)],
        compiler_params=_params(),
        name="rwkv_post",
    )(y, bonus, g, x2d, vec(ln_w), vec(ln_b), w_out.astype(BF16), jm)


def _tile(seq, want):
    return want if seq % want == 0 else seq


def kernel(x, lru_norm, lru_w_in, lru_b_in, lru_conv_w, lru_conv_b, lru_gate_w, lru_gate_b, lru_lambda, lru_w_out, lru_b_out, rwkv_norm, rwkv_mix, rwkv_w_rkv, rwkv_w0, rwkv_w1, rwkv_w2, rwkv_a0, rwkv_a1, rwkv_a2, rwkv_g1, rwkv_g2, rwkv_k_k, rwkv_k_a, rwkv_r_k, rwkv_ln_w, rwkv_ln_b, rwkv_w_out, ffn_norm, ffn_w_up, ffn_conv_w, ffn_conv_b, ffn_w_down, final_norm):
    nb, seq, d = x.shape
    depth = ffn_norm.shape[0]
    tm = _tile(seq, 512)
    tm_pre = _tile(seq, 256)
    nt = seq // tm
    h = x.reshape(nb * seq, d)
    for layer in range(depth):
        j = layer // 2
        if layer % 2 == 0:
            h = _lru(h, lru_norm[j], lru_w_in[j], lru_b_in[j], lru_conv_w[j], lru_conv_b[j],
                     lru_gate_w[j], lru_gate_b[j], lru_lambda[j], lru_w_out[j], lru_b_out[j],
                     nb=nb, nt=nt, tm=tm)
        else:
            h = _rwkv(h, rwkv_norm[j], rwkv_mix[j], rwkv_w_rkv[j], rwkv_w0[j], rwkv_w1[j],
                      rwkv_w2[j], rwkv_a0[j], rwkv_a1[j], rwkv_a2[j], rwkv_g1[j], rwkv_g2[j],
                      rwkv_k_k[j], rwkv_k_a[j], rwkv_r_k[j], rwkv_ln_w[j], rwkv_ln_b[j],
                      rwkv_w_out[j], nb=nb, nt_pre=seq // tm_pre, tm_pre=tm_pre,
                      nt_post=nt, tm_post=tm, seq=seq)
        h = _ffn(h, ffn_norm[layer], ffn_w_up[layer], ffn_conv_w[layer], ffn_conv_b[layer],
                 ffn_w_down[layer], final_norm, nb=nb, nt=nt, tm=tm,
                 final_norm=(layer == depth - 1))
    return h.reshape(nb, seq, d)
```

```python
import functools

import jax
import jax.numpy as jnp
from jax import lax
from jax.experimental import pallas as pl
from jax.experimental.pallas import tpu as pltpu

F32 = jnp.float32
BF16 = jnp.bfloat16

RMS_EPS = 1e-6
GN_EPS = 64e-5
LRU_C = 8.0
HEAD = 64
PAIR = 2 * HEAD
GATE_TILE = 256
SCAN_L = 32
SUBLANES = 8
SCAN_GROUP = 32

VMEM_LIMIT = 56 * 1024 * 1024


def _rms(x, g):
    ms = jnp.mean(x * x, axis=-1, keepdims=True)
    return x * lax.rsqrt(ms + RMS_EPS) * g


def _gelu(x):
    return x * (0.5 * (1.0 + jnp.tanh(0.7978845608028654 * (x + 0.044715 * (x * x * x)))))


def _sigmoid(x):
    return 1.0 / (1.0 + jnp.exp(-x))


def _softplus(x):
    return jnp.maximum(x, 0.0) + jnp.log1p(jnp.exp(-jnp.abs(x)))


def _dot(a, b):
    return jnp.dot(a, b, preferred_element_type=F32)


def _dot_nt(a, b):
    return lax.dot_general(a, b, (((1,), (1,)), ((), ())), preferred_element_type=F32)


def _const_spec(shape):
    nd = len(shape)
    return pl.BlockSpec(shape, lambda *_: (0,) * nd, pipeline_mode=pl.Buffered(1))


def _params():
    return pltpu.CompilerParams(dimension_semantics=("arbitrary", "arbitrary"),
                                vmem_limit_bytes=VMEM_LIMIT)


def _shifted(buf, halo, cur, tm, first):
    @pl.when(first)
    def _():
        halo[...] = jnp.zeros_like(halo)
    buf[0:SUBLANES, :] = halo[...]
    buf[SUBLANES:SUBLANES + tm, :] = cur
    halo[...] = cur[tm - SUBLANES:tm, :]


def _ffn_kernel(x_ref, g_ref, wup_ref, cw_ref, cb_ref, wdn_ref, fn_ref, o_ref,
                ubuf, halo, *, tm, fc, dff, final_norm):
    first = pl.program_id(1) == 0

    @pl.when(first)
    def _():
        halo[...] = jnp.zeros_like(halo)

    x = x_ref[...]
    h = _rms(x, g_ref[...]).astype(BF16)
    acc = jnp.zeros(x.shape, F32)
    for f in range(dff // fc):
        lo = f * fc
        ug = _dot(h, wup_ref[:, lo:lo + fc])
        uv = _dot(h, wup_ref[:, dff + lo:dff + lo + fc])
        ubuf[0:SUBLANES, :] = halo[f]
        ubuf[SUBLANES:SUBLANES + tm, :] = ug
        halo[f] = ug[tm - SUBLANES:tm, :]
        gate = (cb_ref[:, lo:lo + fc]
                + cw_ref[2:3, lo:lo + fc] * ug
                + cw_ref[1:2, lo:lo + fc] * ubuf[SUBLANES - 1:SUBLANES - 1 + tm, :]
                + cw_ref[0:1, lo:lo + fc] * ubuf[SUBLANES - 2:SUBLANES - 2 + tm, :])
        hid = (_gelu(gate) * uv).astype(BF16)
        acc = acc + _dot(hid, wdn_ref[lo:lo + fc, :])
    y = x + acc
    if final_norm:
        y = _rms(y, fn_ref[...])
    o_ref[...] = y


def _ffn(x2d, norm, w_up, conv_w, conv_b, w_down, final_g, *, nb, nt, tm, final_norm):
    m, d = x2d.shape
    dff = w_down.shape[0]
    fc = 512
    row = lambda b, c: (b * nt + c, 0)
    kern = functools.partial(_ffn_kernel, tm=tm, fc=fc, dff=dff, final_norm=final_norm)
    return pl.pallas_call(
        kern,
        grid=(nb, nt),
        in_specs=[
            pl.BlockSpec((tm, d), row),
            _const_spec((1, d)),
            _const_spec((d, 2 * dff)),
            _const_spec((3, dff)),
            _const_spec((1, dff)),
            _const_spec((dff, d)),
            _const_spec((1, d)),
        ],
        out_specs=pl.BlockSpec((tm, d), row),
        out_shape=jax.ShapeDtypeStruct((m, d), F32),
        scratch_shapes=[
            pltpu.VMEM((SUBLANES + tm, fc), F32),
            pltpu.VMEM((dff // fc, SUBLANES, fc), F32),
        ],
        compiler_params=_params(),
        name="conv_ffn",
    )(x2d, norm.reshape(1, d), w_up.astype(BF16), conv_w, conv_b.reshape(1, dff),
      w_down.astype(BF16), final_g.reshape(1, d))


def _lru_kernel(x_ref, g_ref, win_ref, bin_ref, cw_ref, cb_ref, wg_ref, bg_ref, lam_ref,
                wout_ref, bout_ref, o_ref, xbuf, halo, abuf, bbuf, hbuf, hcar, *, tm, d):
    first = pl.program_id(1) == 0

    @pl.when(first)
    def _():
        hcar[...] = jnp.zeros_like(hcar)

    x = x_ref[...]
    h = _rms(x, g_ref[...]).astype(BF16)
    ux = _dot(h, win_ref[:, d:2 * d]) + bin_ref[:, d:2 * d]
    _shifted(xbuf, halo, ux, tm, first)
    xr = (cb_ref[...]
          + cw_ref[3:4, :] * ux
          + cw_ref[2:3, :] * xbuf[SUBLANES - 1:SUBLANES - 1 + tm, :]
          + cw_ref[1:2, :] * xbuf[SUBLANES - 2:SUBLANES - 2 + tm, :]
          + cw_ref[0:1, :] * xbuf[SUBLANES - 3:SUBLANES - 3 + tm, :])

    sp = _softplus(-lam_ref[...])
    sub = lax.broadcasted_iota(jnp.int32, (tm, GATE_TILE), 0) & (SUBLANES - 1)
    for q in range(d // GATE_TILE):
        lo = q * GATE_TILE
        xq = xr[:, lo:lo + GATE_TILE]
        xqb = xq.astype(BF16)
        rg = _sigmoid(_dot(xqb, wg_ref[0, q]) + bg_ref[0:1, lo:lo + GATE_TILE])
        ig = _sigmoid(_dot(xqb, wg_ref[1, q]) + bg_ref[1:2, lo:lo + GATE_TILE])
        log_a = (-LRU_C) * rg * sp[:, lo:lo + GATE_TILE]
        a = jnp.exp(log_a)
        th = jnp.tanh(log_a)
        bt = jnp.sqrt(-2.0 * th / (1.0 - th)) * (ig * xq)
        for s in (1, 2, 4):
            keep = sub >= s
            a_sh = pltpu.roll(a, s, 0)
            b_sh = pltpu.roll(bt, s, 0)
            bt = jnp.where(keep, a * b_sh + bt, bt)
            a = jnp.where(keep, a * a_sh, a)
        abuf[:, lo:lo + GATE_TILE] = a
        bbuf[:, lo:lo + GATE_TILE] = bt

    def carry_group(i, hprev):
        r0 = pl.multiple_of(i * SUBLANES, SUBLANES)
        h8 = abuf[pl.ds(r0, SUBLANES), :] * hprev + bbuf[pl.ds(r0, SUBLANES), :]
        hbuf[pl.ds(r0, SUBLANES), :] = h8
        return h8[SUBLANES - 1:SUBLANES, :]

    hlast = lax.fori_loop(0, tm // SUBLANES, carry_group, hcar[0:1, :], unroll=8)
    hcar[0:1, :] = hlast

    uy = _dot(h, win_ref[:, 0:d]) + bin_ref[:, 0:d]
    out = (hbuf[...] * _gelu(uy)).astype(BF16)
    o_ref[...] = x + _dot(out, wout_ref[...]) + bout_ref[...]


def _lru(x2d, norm, w_in, b_in, conv_w, conv_b, gate_w, gate_b, lam, w_out, b_out, *, nb, nt, tm):
    m, d = x2d.shape
    nblk, bw = gate_w.shape[1], gate_w.shape[2]
    per = GATE_TILE // bw
    w5 = gate_w.reshape(2, nblk // per, per, bw, bw)
    eye = jnp.eye(per, dtype=gate_w.dtype)
    wg = jnp.einsum('gqmcd,mn->gqmcnd', w5, eye).reshape(2, nblk // per, GATE_TILE, GATE_TILE)
    row = lambda b, c: (b * nt + c, 0)
    kern = functools.partial(_lru_kernel, tm=tm, d=d)
    return pl.pallas_call(
        kern,
        grid=(nb, nt),
        in_specs=[
            pl.BlockSpec((tm, d), row),
            _const_spec((1, d)),
            _const_spec((d, 2 * d)),
            _const_spec((1, 2 * d)),
            _const_spec((4, d)),
            _const_spec((1, d)),
            _const_spec((2, d // GATE_TILE, GATE_TILE, GATE_TILE)),
            _const_spec((2, d)),
            _const_spec((1, d)),
            _const_spec((d, d)),
            _const_spec((1, d)),
        ],
        out_specs=pl.BlockSpec((tm, d), row),
        out_shape=jax.ShapeDtypeStruct((m, d), F32),
        scratch_shapes=[
            pltpu.VMEM((SUBLANES + tm, d), F32),
            pltpu.VMEM((SUBLANES, d), F32),
            pltpu.VMEM((tm, d), F32),
            pltpu.VMEM((tm, d), F32),
            pltpu.VMEM((tm, d), F32),
            pltpu.VMEM((SUBLANES, d), F32),
        ],
        compiler_params=_params(),
        name="rglru",
    )(x2d, norm.reshape(1, d), w_in.astype(BF16), b_in.reshape(1, 2 * d), conv_w,
      conv_b.reshape(1, d), wg.astype(BF16), gate_b.reshape(2, d), lam.reshape(1, d),
      w_out.astype(BF16), b_out.reshape(1, d))


def _head_sum(x, jmat):
    hi = x.astype(BF16)
    lo = (x - hi.astype(F32)).astype(BF16)
    return _dot(hi, jmat) + _dot(lo, jmat)


def _pre_kernel(x_ref, g_ref, mix_ref, wrkv_ref, w0_ref, w1_ref, w2_ref, a0_ref, a1_ref, a2_ref,
                g1_ref, g2_ref, kk_ref, ka_ref, rk_ref, jm_ref,
                r_out, c_out, k_out, v_out, kn_out, b_out, g_out, bonus_out,
                hbuf, halo, *, tm, d):
    first = pl.program_id(1) == 0
    x = x_ref[...]
    h = _rms(x, g_ref[...])
    _shifted(hbuf, halo, h, tm, first)
    xx = hbuf[SUBLANES - 1:SUBLANES - 1 + tm, :] - h

    def mixed(i):
        return (h + xx * mix_ref[i:i + 1, :]).astype(BF16)

    r = _dot(mixed(0), wrkv_ref[0])
    k = _dot(mixed(1), wrkv_ref[1])
    v = _dot(mixed(2), wrkv_ref[2])
    wl = _dot(jnp.tanh(_dot(mixed(3), w1_ref[...])).astype(BF16), w2_ref[...])
    w = -_softplus(-(w0_ref[...] + wl)) - 0.5
    lw = -jnp.exp(w)
    a = _sigmoid(a0_ref[...] + _dot(_dot(mixed(4), a1_ref[...]).astype(BF16), a2_ref[...]))
    g_out[...] = _dot(_sigmoid(_dot(mixed(5), g1_ref[...])).astype(BF16), g2_ref[...])

    tin = lax.broadcasted_iota(jnp.int32, (tm, d), 0) & (SCAN_L - 1)
    c = lw
    s = 1
    while s < SCAN_L:
        c = jnp.where(tin >= s, c + pltpu.roll(c, s, 0), c)
        s *= 2

    kf = k * (1.0 + (a - 1.0) * ka_ref[...])
    kk = k * kk_ref[...]
    rkk = r * kf * rk_ref[...]
    jm = jm_ref[...]
    for p in range(d // PAIR):
        sl = slice(p * PAIR, (p + 1) * PAIR)
        kkp = kk[:, sl]
        nrm = jnp.sqrt(_head_sum(kkp * kkp, jm))
        kn = kkp / jnp.maximum(nrm, 1e-12)
        kn_out[p] = kn
        b_out[p] = kn * a[:, sl]
        bonus_out[:, sl] = _head_sum(rkk[:, sl], jm) * v[:, sl]
        r_out[p] = r[:, sl]
        c_out[p] = c[:, sl]
        k_out[p] = kf[:, sl]
        v_out[p] = v[:, sl]


def _scan_kernel(r_ref, c_ref, k_ref, v_ref, kn_ref, b_ref, y_ref, s_ref, *, nu):
    L = SCAN_L

    @pl.when(pl.program_id(1) == 0)
    def _():
        s_ref[...] = jnp.zeros_like(s_ref)

    lane = lax.broadcasted_iota(jnp.int32, (L, PAIR), 1)
    trow = lax.broadcasted_iota(jnp.int32, (L, PAIR), 0)
    h0 = lane < HEAD
    col_t = lane & (L - 1)
    strict = col_t < trow
    incl = col_t <= trow
    nb0 = strict & (lane < L)
    nb1 = strict & (lane >= L) & (lane < 2 * L)
    srow = lax.broadcasted_iota(jnp.int32, (PAIR, PAIR), 0)
    scol = lax.broadcasted_iota(jnp.int32, (PAIR, PAIR), 1)
    same_head = (srow < HEAD) == (scol < HEAD)
    eye2 = (lax.broadcasted_iota(jnp.int32, (2 * L, PAIR), 0)
            == lax.broadcasted_iota(jnp.int32, (2 * L, PAIR), 1)).astype(F32)
    zeros2 = jnp.zeros((2 * L, PAIR), F32)

    def split(t):
        return jnp.concatenate([jnp.where(h0, t, 0.0), jnp.where(h0, 0.0, t)], axis=0)

    def pad_rows(t):
        return jnp.concatenate([t, zeros2], axis=0).astype(BF16)

    def group(gi, carry):
        units = [gi * SCAN_GROUP + j for j in range(SCAN_GROUP)]
        lhs, rhs, vm, vs, bk, gl = [], [], [], [], [], []
        for u in units:
            r = r_ref[u]
            c = c_ref[u]
            k = k_ref[u]
            v = v_ref[u]
            kn = kn_ref[u]
            bv = b_ref[u]
            c_last = c[L - 1:L, :]
            c_excl = jnp.where(trow >= 1, pltpu.roll(c, 1, 0), 0.0)
            e_neg = jnp.exp(-c)
            e_end = jnp.exp(c_last - c)
            rt = r * jnp.exp(c)
            at = -(kn * jnp.exp(c_excl))
            lhs.append(jnp.concatenate([at, rt], axis=0).astype(BF16))
            rhs.append(jnp.concatenate([split(bv * e_neg), split(k * e_neg)], axis=0).astype(BF16))
            vm.append(split(v))
            vs.append(v)
            bk.append(jnp.concatenate([bv * e_end, k * e_end], axis=0).astype(BF16))
            gl.append(jnp.exp(c_last))
        st = [s_ref[u] for u in units]
        ps = [_dot_nt(a, jnp.concatenate([b, s.astype(BF16)], axis=0)) for a, b, s in zip(lhs, rhs, st)]
        ptop = [jnp.where(strict, q[0:L, 0:PAIR], 0.0) for q in ps]
        pbot = [jnp.where(incl, q[L:2 * L, 0:PAIR], 0.0) for q in ps]
        npow = [jnp.concatenate([jnp.where(nb0, q, 0.0), jnp.where(nb1, q, 0.0)], axis=0) for q in ptop]
        wak = [_dot(q.astype(BF16), jnp.concatenate([zeros2, m], axis=0).astype(BF16))
               for q, m in zip(ptop, vm)]
        tinv = [eye2 + n for n in npow]
        for _ in range(4):
            npow = [_dot(n.astype(BF16), pad_rows(n)) for n in npow]
            tinv = [t + _dot(t.astype(BF16), pad_rows(n)) for t, n in zip(tinv, npow)]
        w = [q[0:L, PAIR:2 * PAIR] + a for q, a in zip(ps, wak)]
        x = [_dot(t.astype(BF16), pad_rows(split(a))) for t, a in zip(tinv, w)]
        for u, q, pb, xx, m in zip(units, ps, pbot, x, vm):
            y_ref[u] = q[L:2 * L, PAIR:2 * PAIR] + _dot(
                pb.astype(BF16), jnp.concatenate([xx, m], axis=0).astype(BF16))
        for u, s, xx, v, b, g in zip(units, st, x, vs, bk, gl):
            uu = xx[0:L] + xx[L:2 * L]
            uv_t = jnp.concatenate([uu, v], axis=0).T.astype(BF16)
            s_ref[u] = jnp.where(same_head, s * g + _dot(uv_t, b), 0.0)
        return carry

    lax.fori_loop(0, nu // SCAN_GROUP, group, 0)


def _post_kernel(y_ref, bonus_ref, g_ref, x_ref, lnw_ref, lnb_ref, wout_ref, jm_ref, o_ref,
                 zbuf, *, d):
    jm = jm_ref[...]
    for p in range(d // PAIR):
        sl = slice(p * PAIR, (p + 1) * PAIR)
        y = y_ref[p]
        mu = _head_sum(y, jm) * (1.0 / HEAD)
        dy = y - mu
        var = _head_sum(dy * dy, jm) * (1.0 / HEAD)
        yn = dy * lax.rsqrt(var + GN_EPS) * lnw_ref[:, sl] + lnb_ref[:, sl]
        zbuf[:, sl] = ((yn + bonus_ref[:, sl]) * g_ref[:, sl]).astype(BF16)
    o_ref[...] = x_ref[...] + _dot(zbuf[...], wout_ref[...])


def _rwkv(x2d, norm, mix, w_rkv, w0, w1, w2, a0, a1, a2, g1, g2, k_k, k_a, r_k, ln_w, ln_b, w_out,
          *, nb, nt_pre, tm_pre, nt_post, tm_post, seq):
    m, d = x2d.shape
    npair = d // PAIR
    assert 4 * SCAN_L == PAIR and tm_pre % SCAN_L == 0
    idx = jnp.arange(PAIR) // HEAD
    jm = (idx[:, None] == idx[None, :]).astype(BF16)
    vec = lambda t: t.reshape(1, d)
    row_pre = lambda b, c: (b * nt_pre + c, 0)
    pair_pre = lambda b, c: (b, c, 0)
    pair_shape = jax.ShapeDtypeStruct((nb * npair, seq, PAIR), F32)
    pair_spec = pl.BlockSpec((npair, tm_pre, PAIR), pair_pre)
    flat_shape = jax.ShapeDtypeStruct((m, d), F32)
    r, c, k, v, kn, bv, g, bonus = pl.pallas_call(
        functools.partial(_pre_kernel, tm=tm_pre, d=d),
        grid=(nb, nt_pre),
        in_specs=[
            pl.BlockSpec((tm_pre, d), row_pre),
            _const_spec((1, d)),
            _const_spec((6, d)),
            _const_spec((3, d, d)),
            _const_spec((1, d)),
            _const_spec(w1.shape),
            _const_spec(w2.shape),
            _const_spec((1, d)),
            _const_spec(a1.shape),
            _const_spec(a2.shape),
            _const_spec(g1.shape),
            _const_spec(g2.shape),
            _const_spec((1, d)),
            _const_spec((1, d)),
            _const_spec((1, d)),
            _const_spec((PAIR, PAIR)),
        ],
        out_specs=[pair_spec] * 6 + [pl.BlockSpec((tm_pre, d), row_pre)] * 2,
        out_shape=[pair_shape] * 6 + [flat_shape] * 2,
        scratch_shapes=[
            pltpu.VMEM((SUBLANES + tm_pre, d), F32),
            pltpu.VMEM((SUBLANES, d), F32),
        ],
        compiler_params=_params(),
        name="rwkv_pre",
    )(x2d, vec(norm), mix, w_rkv.astype(BF16), vec(w0), w1.astype(BF16), w2.astype(BF16),
      vec(a0), a1.astype(BF16), a2.astype(BF16), g1.astype(BF16), g2.astype(BF16),
      vec(k_k), vec(k_a), vec(r_k), jm)

    nu = 4 * npair
    unit_spec = pl.BlockSpec((nu, SCAN_L, PAIR), lambda ug, c: (ug, c, 0))
    y = pl.pallas_call(
        functools.partial(_scan_kernel, nu=nu),
        grid=(nb * npair // nu, seq // SCAN_L),
        in_specs=[unit_spec] * 6,
        out_specs=unit_spec,
        out_shape=pair_shape,
        scratch_shapes=[pltpu.VMEM((nu, PAIR, PAIR), F32)],
        compiler_params=_params(),
        name="rwkv_scan",
    )(r, c, k, v, kn, bv)

    row_post = lambda b, c: (b * nt_post + c, 0)
    return pl.pallas_call(
        functools.partial(_post_kernel, d=d),
        grid=(nb, nt_post),
        in_specs=[
            pl.BlockSpec((npair, tm_post, PAIR), lambda b, c: (b, c, 0)),
            pl.BlockSpec((tm_post, d), row_post),
            pl.BlockSpec((tm_post, d), row_post),
            pl.BlockSpec((tm_post, d), row_post),
            _const_spec((1, d)),
            _const_spec((1, d)),
            _const_spec((d, d)),
            _const_spec((PAIR, PAIR)),
        ],
        out_specs=pl.BlockSpec((tm_post, d), row_post),
        out_shape=flat_shape,
        scratch_shapes=[pltpu.VMEM((tm_post, d), BF16)],
        compiler_params=_params(),
        name="rwkv_post",
    )(y, bonus, g, x2d, vec(ln_w), vec(ln_b), w_out.astype(BF16), jm)


def _tile(seq, want):
    return want if seq % want == 0 else seq


def kernel(x, lru_norm, lru_w_in, lru_b_in, lru_conv_w, lru_conv_b, lru_gate_w, lru_gate_b, lru_lambda, lru_w_out, lru_b_out, rwkv_norm, rwkv_mix, rwkv_w_rkv, rwkv_w0, rwkv_w1, rwkv_w2, rwkv_a0, rwkv_a1, rwkv_a2, rwkv_g1, rwkv_g2, rwkv_k_k, rwkv_k_a, rwkv_r_k, rwkv_ln_w, rwkv_ln_b, rwkv_w_out, ffn_norm, ffn_w_up, ffn_conv_w, ffn_conv_b, ffn_w_down, final_norm):
    nb, seq, d = x.shape
    depth = ffn_norm.shape[0]
    tm = _tile(seq, 512)
    tm_pre = _tile(seq, 256)
    nt = seq // tm
    h = x.reshape(nb * seq, d)
    for layer in range(depth):
        j = layer // 2
        if layer % 2 == 0:
            h = _lru(h, lru_norm[j], lru_w_in[j], lru_b_in[j], lru_conv_w[j], lru_conv_b[j],
                     lru_gate_w[j], lru_gate_b[j], lru_lambda[j], lru_w_out[j], lru_b_out[j],
                     nb=nb, nt=nt, tm=tm)
        else:
            h = _rwkv(h, rwkv_norm[j], rwkv_mix[j], rwkv_w_rkv[j], rwkv_w0[j], rwkv_w1[j],
                      rwkv_w2[j], rwkv_a0[j], rwkv_a1[j], rwkv_a2[j], rwkv_g1[j], rwkv_g2[j],
                      rwkv_k_k[j], rwkv_k_a[j], rwkv_r_k[j], rwkv_ln_w[j], rwkv_ln_b[j],
                      rwkv_w_out[j], nb=nb, nt_pre=seq // tm_pre, tm_pre=tm_pre,
                      nt_post=nt, tm_post=tm, seq=seq)
        h = _ffn(h, ffn_norm[layer], ffn_w_up[layer], ffn_conv_w[layer], ffn_conv_b[layer],
                 ffn_w_down[layer], final_norm, nb=nb, nt=nt, tm=tm,
                 final_norm=(layer == depth - 1))
    return h.reshape(nb, seq, d)
```

```python
import functools

import jax
import jax.numpy as jnp
from jax import lax
from jax.experimental import pallas as pl
from jax.experimental.pallas import tpu as pltpu

F32 = jnp.float32
BF16 = jnp.bfloat16

RMS_EPS = 1e-6
GN_EPS = 64e-5
LRU_C = 8.0
HEAD = 64
PAIR = 2 * HEAD
GATE_TILE = 256
SCAN_L = 32
SUBLANES = 8
LANES = 128
BF16_ROWS = 16
SCAN_GROUP = 32

VMEM_LIMIT = 56 * 1024 * 1024


def _rms(x, g):
    ms = jnp.mean(x * x, axis=-1, keepdims=True)
    return x * lax.rsqrt(ms + RMS_EPS) * g


def _gelu(x):
    return x * (0.5 * (1.0 + jnp.tanh(0.7978845608028654 * (x + 0.044715 * (x * x * x)))))


def _sigmoid(x):
    return 0.5 * jnp.tanh(0.5 * x) + 0.5


def _sqrt_nonneg(x):
    return jnp.where(x > 0.0, x * lax.rsqrt(x), 0.0)


def _softplus(x):
    return jnp.maximum(x, 0.0) + jnp.log1p(jnp.exp(-jnp.abs(x)))


def _dot(a, b):
    return jnp.dot(a, b, preferred_element_type=F32)


def _dot_nt(a, b):
    return lax.dot_general(a, b, (((1,), (1,)), ((), ())), preferred_element_type=F32)


def _const_spec(shape):
    nd = len(shape)
    return pl.BlockSpec(shape, lambda *_: (0,) * nd, pipeline_mode=pl.Buffered(1))


def _params():
    return pltpu.CompilerParams(dimension_semantics=("arbitrary", "arbitrary"),
                                vmem_limit_bytes=VMEM_LIMIT)


def _shifted(buf, halo, cur, tm, first):
    @pl.when(first)
    def _():
        halo[...] = jnp.zeros_like(halo)
    buf[0:SUBLANES, :] = halo[...]
    buf[SUBLANES:SUBLANES + tm, :] = cur
    halo[...] = cur[tm - SUBLANES:tm, :]


def _ffn_kernel(x_ref, g_ref, wup_ref, cw_ref, cb_ref, wdn_ref, fn_ref, o_ref,
                ubuf, halo, hid_buf, *, tm, fc, dff, final_norm):
    first = pl.program_id(1) == 0

    @pl.when(first)
    def _():
        halo[...] = jnp.zeros_like(halo)

    x = x_ref[...]
    h = _rms(x, g_ref[...]).astype(BF16)
    for f in range(dff // fc):
        lo = f * fc
        ug = _dot(h, wup_ref[:, lo:lo + fc])
        uv = _dot(h, wup_ref[:, dff + lo:dff + lo + fc])
        ubuf[0:SUBLANES, :] = halo[f]
        ubuf[SUBLANES:SUBLANES + tm, :] = ug
        halo[f] = ug[tm - SUBLANES:tm, :]
        gate = (cb_ref[:, lo:lo + fc]
                + cw_ref[2:3, lo:lo + fc] * ug
                + cw_ref[1:2, lo:lo + fc] * ubuf[SUBLANES - 1:SUBLANES - 1 + tm, :]
                + cw_ref[0:1, lo:lo + fc] * ubuf[SUBLANES - 2:SUBLANES - 2 + tm, :])
        hid_buf[:, lo:lo + fc] = (_gelu(gate) * uv).astype(BF16)
    y = x + _dot(hid_buf[...], wdn_ref[...])
    if final_norm:
        y = _rms(y, fn_ref[...])
    o_ref[...] = y


def _ffn(x2d, norm, w_up, conv_w, conv_b, w_down, final_g, *, nb, nt, tm, final_norm):
    m, d = x2d.shape
    dff = w_down.shape[0]
    fc = 256
    row = lambda b, c: (b * nt + c, 0)
    kern = functools.partial(_ffn_kernel, tm=tm, fc=fc, dff=dff, final_norm=final_norm)
    return pl.pallas_call(
        kern,
        grid=(nb, nt),
        in_specs=[
            pl.BlockSpec((tm, d), row),
            _const_spec((1, d)),
            _const_spec((d, 2 * dff)),
            _const_spec((3, dff)),
            _const_spec((1, dff)),
            _const_spec((dff, d)),
            _const_spec((1, d)),
        ],
        out_specs=pl.BlockSpec((tm, d), row),
        out_shape=jax.ShapeDtypeStruct((m, d), F32),
        scratch_shapes=[
            pltpu.VMEM((SUBLANES + tm, fc), F32),
            pltpu.VMEM((dff // fc, SUBLANES, fc), F32),
            pltpu.VMEM((tm, dff), BF16),
        ],
        compiler_params=_params(),
        name="conv_ffn",
    )(x2d, norm.reshape(1, d), w_up.astype(BF16), conv_w, conv_b.reshape(1, dff),
      w_down.astype(BF16), final_g.reshape(1, d))


def _lru_kernel(x_ref, g_ref, win_ref, bin_ref, cw_ref, cb_ref, wg_ref, bg_ref, lam_ref,
                wout_ref, bout_ref, o_ref, xbuf, halo, abuf, bbuf, hbuf, hcar, *, tm, d):
    first = pl.program_id(1) == 0

    @pl.when(first)
    def _():
        hcar[...] = jnp.zeros_like(hcar)

    x = x_ref[...]
    h = _rms(x, g_ref[...]).astype(BF16)
    ux = _dot(h, win_ref[:, d:2 * d]) + bin_ref[:, d:2 * d]
    _shifted(xbuf, halo, ux, tm, first)
    xr = (cb_ref[...]
          + cw_ref[3:4, :] * ux
          + cw_ref[2:3, :] * xbuf[SUBLANES - 1:SUBLANES - 1 + tm, :]
          + cw_ref[1:2, :] * xbuf[SUBLANES - 2:SUBLANES - 2 + tm, :]
          + cw_ref[0:1, :] * xbuf[SUBLANES - 3:SUBLANES - 3 + tm, :])

    sp = _softplus(-lam_ref[...])
    sub = lax.broadcasted_iota(jnp.int32, (tm // SUBLANES, SUBLANES, GATE_TILE), 1)
    for q in range(d // GATE_TILE):
        lo = q * GATE_TILE
        xq = xr[:, lo:lo + GATE_TILE]
        xqb = xq.astype(BF16)
        rg = _sigmoid(_dot(xqb, wg_ref[0, q]) + bg_ref[0:1, lo:lo + GATE_TILE])
        ig = _sigmoid(_dot(xqb, wg_ref[1, q]) + bg_ref[1:2, lo:lo + GATE_TILE])
        log_a = (-LRU_C) * rg * sp[:, lo:lo + GATE_TILE]
        a = jnp.exp(log_a)
        bt = (1.0 + a) * _sqrt_nonneg(-jnp.tanh(0.5 * log_a)) * (ig * xq)
        a3 = a.reshape(tm // SUBLANES, SUBLANES, GATE_TILE)
        b3 = bt.reshape(tm // SUBLANES, SUBLANES, GATE_TILE)
        for s in (1, 2, 4):
            keep = sub >= s
            b3 = a3 * jnp.where(keep, pltpu.roll(b3, s, 1), 0.0) + b3
            a3 = a3 * jnp.where(keep, pltpu.roll(a3, s, 1), 1.0)
        abuf[:, lo:lo + GATE_TILE] = a3.reshape(tm, GATE_TILE)
        bbuf[:, lo:lo + GATE_TILE] = b3.reshape(tm, GATE_TILE)

    def carry_group(i, hprev):
        r0 = pl.multiple_of(i * SUBLANES, SUBLANES)
        h8 = abuf[pl.ds(r0, SUBLANES), :] * hprev + bbuf[pl.ds(r0, SUBLANES), :]
        hbuf[pl.ds(r0, SUBLANES), :] = h8
        return h8[SUBLANES - 1:SUBLANES, :]

    hlast = lax.fori_loop(0, tm // SUBLANES, carry_group, hcar[0:1, :], unroll=8)
    hcar[0:1, :] = hlast

    uy = _dot(h, win_ref[:, 0:d]) + bin_ref[:, 0:d]
    out = (hbuf[...] * _gelu(uy)).astype(BF16)
    o_ref[...] = x + _dot(out, wout_ref[...]) + bout_ref[...]


def _lru(x2d, norm, w_in, b_in, conv_w, conv_b, gate_w, gate_b, lam, w_out, b_out, *, nb, nt, tm):
    m, d = x2d.shape
    nblk, bw = gate_w.shape[1], gate_w.shape[2]
    per = GATE_TILE // bw
    w5 = gate_w.reshape(2, nblk // per, per, bw, bw)
    eye = jnp.eye(per, dtype=gate_w.dtype)
    wg = jnp.einsum('gqmcd,mn->gqmcnd', w5, eye).reshape(2, nblk // per, GATE_TILE, GATE_TILE)
    row = lambda b, c: (b * nt + c, 0)
    kern = functools.partial(_lru_kernel, tm=tm, d=d)
    return pl.pallas_call(
        kern,
        grid=(nb, nt),
        in_specs=[
            pl.BlockSpec((tm, d), row),
            _const_spec((1, d)),
            _const_spec((d, 2 * d)),
            _const_spec((1, 2 * d)),
            _const_spec((4, d)),
            _const_spec((1, d)),
            _const_spec((2, d // GATE_TILE, GATE_TILE, GATE_TILE)),
            _const_spec((2, d)),
            _const_spec((1, d)),
            _const_spec((d, d)),
            _const_spec((1, d)),
        ],
        out_specs=pl.BlockSpec((tm, d), row),
        out_shape=jax.ShapeDtypeStruct((m, d), F32),
        scratch_shapes=[
            pltpu.VMEM((SUBLANES + tm, d), F32),
            pltpu.VMEM((SUBLANES, d), F32),
            pltpu.VMEM((tm, d), F32),
            pltpu.VMEM((tm, d), F32),
            pltpu.VMEM((tm, d), F32),
            pltpu.VMEM((SUBLANES, d), F32),
        ],
        compiler_params=_params(),
        name="rglru",
    )(x2d, norm.reshape(1, d), w_in.astype(BF16), b_in.reshape(1, 2 * d), conv_w,
      conv_b.reshape(1, d), wg.astype(BF16), gate_b.reshape(2, d), lam.reshape(1, d),
      w_out.astype(BF16), b_out.reshape(1, d))


def _head_sum(x, jmat):
    hi = x.astype(BF16)
    lo = (x - hi.astype(F32)).astype(BF16)
    return _dot(hi, jmat) + _dot(lo, jmat)


def _pre_kernel(x_ref, g_ref, mix_ref, wrkv_ref, w0_ref, w1_ref, w2_ref, a0_ref, a1_ref, a2_ref,
                g1_ref, g2_ref, kk_ref, ka_ref, rk_ref, jm_ref,
                r_out, c_out, k_out, v_out, kn_out, b_out, g_out, bonus_out,
                hbuf, halo, *, tm, d):
    first = pl.program_id(1) == 0
    x = x_ref[...]
    h = _rms(x, g_ref[...])
    _shifted(hbuf, halo, h, tm, first)
    xx = hbuf[SUBLANES - 1:SUBLANES - 1 + tm, :] - h

    hb = h.astype(BF16).reshape(tm // BF16_ROWS, BF16_ROWS, d)
    xxb = xx.astype(BF16).reshape(tm // BF16_ROWS, BF16_ROWS, d)

    def mixed(i):
        return (hb + xxb * mix_ref[i]).reshape(tm, d)

    r = _dot(mixed(0), wrkv_ref[0])
    k = _dot(mixed(1), wrkv_ref[1])
    v = _dot(mixed(2), wrkv_ref[2])
    wl = _dot(jnp.tanh(_dot(mixed(3), w1_ref[...])).astype(BF16), w2_ref[...])
    z = -(w0_ref[...] + wl)
    w = -(jnp.maximum(z, 0.0) + jnp.log(1.0 + jnp.exp(-jnp.abs(z)))) - 0.5
    lw = -jnp.exp(w)
    a = _sigmoid(a0_ref[...] + _dot(_dot(mixed(4), a1_ref[...]).astype(BF16), a2_ref[...]))
    g_out[...] = _dot(_sigmoid(_dot(mixed(5), g1_ref[...])).astype(BF16), g2_ref[...])

    tin = lax.broadcasted_iota(jnp.int32, (tm, d), 0) & (SCAN_L - 1)
    c = lw
    s = 1
    while s < SCAN_L:
        c = jnp.where(tin >= s, c + pltpu.roll(c, s, 0), c)
        s *= 2

    kf = k * (1.0 + (a - 1.0) * ka_ref[...])
    kk = k * kk_ref[...]
    rkk = r * kf * rk_ref[...]
    jm = jm_ref[...]
    for p in range(d // PAIR):
        sl = slice(p * PAIR, (p + 1) * PAIR)
        kkp = kk[:, sl]
        kn = kkp * jnp.minimum(lax.rsqrt(_head_sum(kkp * kkp, jm)), 1e12)
        kn_out[p] = kn
        b_out[p] = kn * a[:, sl]
        bonus_out[:, sl] = _head_sum(rkk[:, sl], jm) * v[:, sl]
        r_out[p] = r[:, sl]
        c_out[p] = c[:, sl]
        k_out[p] = kf[:, sl]
        v_out[p] = v[:, sl]


def _scan_kernel(r_ref, c_ref, k_ref, v_ref, kn_ref, b_ref, y_ref, s_ref, *, nu):
    L = SCAN_L

    @pl.when(pl.program_id(1) == 0)
    def _():
        s_ref[...] = jnp.zeros_like(s_ref)

    lane = lax.broadcasted_iota(jnp.int32, (L, PAIR), 1)
    trow = lax.broadcasted_iota(jnp.int32, (L, PAIR), 0)
    h0 = lane < HEAD
    col_t = lane & (L - 1)
    strict = col_t < trow
    incl = col_t <= trow
    nb0 = strict & (lane < L)
    nb1 = strict & (lane >= L) & (lane < 2 * L)
    srow = lax.broadcasted_iota(jnp.int32, (PAIR, PAIR), 0)
    scol = lax.broadcasted_iota(jnp.int32, (PAIR, PAIR), 1)
    same_head = (srow < HEAD) == (scol < HEAD)
    eye2 = (lax.broadcasted_iota(jnp.int32, (2 * L, PAIR), 0)
            == lax.broadcasted_iota(jnp.int32, (2 * L, PAIR), 1)).astype(F32)
    zeros2 = jnp.zeros((2 * L, PAIR), F32)

    def split(t):
        return jnp.concatenate([jnp.where(h0, t, 0.0), jnp.where(h0, 0.0, t)], axis=0)

    def pad_rows(t):
        return jnp.concatenate([t, zeros2], axis=0).astype(BF16)

    def group(gi, carry):
        units = [gi * SCAN_GROUP + j for j in range(SCAN_GROUP)]
        lhs, rhs, vm, vs, bk, gl = [], [], [], [], [], []
        for u in units:
            r = r_ref[u]
            c = c_ref[u]
            k = k_ref[u]
            v = v_ref[u]
            kn = kn_ref[u]
            bv = b_ref[u]
            c_last = c[L - 1:L, :]
            c_excl = jnp.where(trow >= 1, pltpu.roll(c, 1, 0), 0.0)
            e_neg = jnp.exp(-c)
            e_end = jnp.exp(c_last - c)
            rt = r * jnp.exp(c)
            at = -(kn * jnp.exp(c_excl))
            lhs.append(jnp.concatenate([at, rt], axis=0).astype(BF16))
            rhs.append(jnp.concatenate([split(bv * e_neg), split(k * e_neg)], axis=0).astype(BF16))
            vm.append(split(v))
            vs.append(v)
            bk.append(jnp.concatenate([bv * e_end, k * e_end], axis=0).astype(BF16))
            gl.append(jnp.exp(c_last))
        st = [s_ref[u] for u in units]
        ps = [_dot_nt(a, jnp.concatenate([b, s.astype(BF16)], axis=0)) for a, b, s in zip(lhs, rhs, st)]
        ptop = [jnp.where(strict, q[0:L, 0:PAIR], 0.0) for q in ps]
        pbot = [jnp.where(incl, q[L:2 * L, 0:PAIR], 0.0) for q in ps]
        npow = [jnp.concatenate([jnp.where(nb0, q, 0.0), jnp.where(nb1, q, 0.0)], axis=0) for q in ptop]
        wak = [_dot(q.astype(BF16), jnp.concatenate([zeros2, m], axis=0).astype(BF16))
               for q, m in zip(ptop, vm)]
        tinv = [eye2 + n for n in npow]
        for _ in range(4):
            npow = [_dot(n.astype(BF16), pad_rows(n)) for n in npow]
            tinv = [t + _dot(t.astype(BF16), pad_rows(n)) for t, n in zip(tinv, npow)]
        w = [q[0:L, PAIR:2 * PAIR] + a for q, a in zip(ps, wak)]
        x = [_dot(t.astype(BF16), pad_rows(split(a))) for t, a in zip(tinv, w)]
        for u, q, pb, xx, m in zip(units, ps, pbot, x, vm):
            y_ref[u] = q[L:2 * L, PAIR:2 * PAIR] + _dot(
                pb.astype(BF16), jnp.concatenate([xx, m], axis=0).astype(BF16))
        for u, s, xx, v, b, g in zip(units, st, x, vs, bk, gl):
            uu = xx[0:L] + xx[L:2 * L]
            uv_t = jnp.concatenate([uu, v], axis=0).T.astype(BF16)
            s_ref[u] = jnp.where(same_head, s * g + _dot(uv_t, b), 0.0)
        return carry

    lax.fori_loop(0, nu // SCAN_GROUP, group, 0)


def _post_kernel(y_ref, bonus_ref, g_ref, x_ref, lnw_ref, lnb_ref, wout_ref, jm_ref, o_ref,
                 zbuf, *, d):
    jm = jm_ref[...]
    for p in range(d // PAIR):
        sl = slice(p * PAIR, (p + 1) * PAIR)
        y = y_ref[p]
        mu = _head_sum(y, jm) * (1.0 / HEAD)
        dy = y - mu
        var = _head_sum(dy * dy, jm) * (1.0 / HEAD)
        yn = dy * lax.rsqrt(var + GN_EPS) * lnw_ref[:, sl] + lnb_ref[:, sl]
        zbuf[:, sl] = ((yn + bonus_ref[:, sl]) * g_ref[:, sl]).astype(BF16)
    o_ref[...] = x_ref[...] + _dot(zbuf[...], wout_ref[...])


def _rwkv(x2d, norm, mix, w_rkv, w0, w1, w2, a0, a1, a2, g1, g2, k_k, k_a, r_k, ln_w, ln_b, w_out,
          *, nb, nt_pre, tm_pre, nt_post, tm_post, seq):
    m, d = x2d.shape
    npair = d // PAIR
    assert 4 * SCAN_L == PAIR and tm_pre % SCAN_L == 0
    idx = jnp.arange(PAIR) // HEAD
    jm = (idx[:, None] == idx[None, :]).astype(BF16)
    vec = lambda t: t.reshape(1, d)
    row_pre = lambda b, c: (b * nt_pre + c, 0)
    pair_pre = lambda b, c: (b, c, 0)
    pair_shape = jax.ShapeDtypeStruct((nb * npair, seq, PAIR), F32)
    pair_spec = pl.BlockSpec((npair, tm_pre, PAIR), pair_pre)
    flat_shape = jax.ShapeDtypeStruct((m, d), F32)
    r, c, k, v, kn, bv, g, bonus = pl.pallas_call(
        functools.partial(_pre_kernel, tm=tm_pre, d=d),
        grid=(nb, nt_pre),
        in_specs=[
            pl.BlockSpec((tm_pre, d), row_pre),
            _const_spec((1, d)),
            _const_spec((6, BF16_ROWS, d)),
            _const_spec((3, d, d)),
            _const_spec((1, d)),
            _const_spec(w1.shape),
            _const_spec(w2.shape),
            _const_spec((1, d)),
            _const_spec(a1.shape),
            _const_spec(a2.shape),
            _const_spec(g1.shape),
            _const_spec(g2.shape),
            _const_spec((1, d)),
            _const_spec((1, d)),
            _const_spec((1, d)),
            _const_spec((PAIR, PAIR)),
        ],
        out_specs=[pair_spec] * 6 + [pl.BlockSpec((tm_pre, d), row_pre)] * 2,
        out_shape=[pair_shape] * 6 + [flat_shape] * 2,
        scratch_shapes=[
            pltpu.VMEM((SUBLANES + tm_pre, d), F32),
            pltpu.VMEM((SUBLANES, d), F32),
        ],
        compiler_params=_params(),
        name="rwkv_pre",
    )(x2d, vec(norm), jnp.broadcast_to(mix[:, None, :], (6, BF16_ROWS, d)).astype(BF16),
      w_rkv.astype(BF16), vec(w0), w1.astype(BF16), w2.astype(BF16),
      vec(a0), a1.astype(BF16), a2.astype(BF16), g1.astype(BF16), g2.astype(BF16),
      vec(k_k), vec(k_a), vec(r_k), jm)

    nu = 4 * npair
    unit_spec = pl.BlockSpec((nu, SCAN_L, PAIR), lambda ug, c: (ug, c, 0))
    y = pl.pallas_call(
        functools.partial(_scan_kernel, nu=nu),
        grid=(nb * npair // nu, seq // SCAN_L),
        in_specs=[unit_spec] * 6,
        out_specs=unit_spec,
        out_shape=pair_shape,
        scratch_shapes=[pltpu.VMEM((nu, PAIR, PAIR), F32)],
        compiler_params=_params(),
        name="rwkv_scan",
    )(r, c, k, v, kn, bv)

    row_post = lambda b, c: (b * nt_post + c, 0)
    return pl.pallas_call(
        functools.partial(_post_kernel, d=d),
        grid=(nb, nt_post),
        in_specs=[
            pl.BlockSpec((npair, tm_post, PAIR), lambda b, c: (b, c, 0)),
            pl.BlockSpec((tm_post, d), row_post),
            pl.BlockSpec((tm_post, d), row_post),
            pl.BlockSpec((tm_post, d), row_post),
            _const_spec((1, d)),
            _const_spec((1, d)),
            _const_spec((d, d)),
            _const_spec((PAIR, PAIR)),
        ],
        out_specs=pl.BlockSpec((tm_post, d), row_post),
        out_shape=flat_shape,
        scratch_shapes=[pltpu.VMEM((tm_post, d), BF16)],
        compiler_params=_params(),
        name="rwkv_post",
    )(y, bonus, g, x2d, vec(ln_w), vec(ln_b), w_out.astype(BF16), jm)


def _tile(seq, want):
    return want if seq % want == 0 else seq


def kernel(x, lru_norm, lru_w_in, lru_b_in, lru_conv_w, lru_conv_b, lru_gate_w, lru_gate_b, lru_lambda, lru_w_out, lru_b_out, rwkv_norm, rwkv_mix, rwkv_w_rkv, rwkv_w0, rwkv_w1, rwkv_w2, rwkv_a0, rwkv_a1, rwkv_a2, rwkv_g1, rwkv_g2, rwkv_k_k, rwkv_k_a, rwkv_r_k, rwkv_ln_w, rwkv_ln_b, rwkv_w_out, ffn_norm, ffn_w_up, ffn_conv_w, ffn_conv_b, ffn_w_down, final_norm):
    nb, seq, d = x.shape
    depth = ffn_norm.shape[0]
    tm = _tile(seq, 512)
    tm_pre = _tile(seq, 256)
    tm_ffn = _tile(seq, 1024)
    nt = seq // tm
    h = x.reshape(nb * seq, d)
    for layer in range(depth):
        j = layer // 2
        if layer % 2 == 0:
            h = _lru(h, lru_norm[j], lru_w_in[j], lru_b_in[j], lru_conv_w[j], lru_conv_b[j],
                     lru_gate_w[j], lru_gate_b[j], lru_lambda[j], lru_w_out[j], lru_b_out[j],
                     nb=nb, nt=nt, tm=tm)
        else:
            h = _rwkv(h, rwkv_norm[j], rwkv_mix[j], rwkv_w_rkv[j], rwkv_w0[j], rwkv_w1[j],
                      rwkv_w2[j], rwkv_a0[j], rwkv_a1[j], rwkv_a2[j], rwkv_g1[j], rwkv_g2[j],
                      rwkv_k_k[j], rwkv_k_a[j], rwkv_r_k[j], rwkv_ln_w[j], rwkv_ln_b[j],
                      rwkv_w_out[j], nb=nb, nt_pre=seq // tm_pre, tm_pre=tm_pre,
                      nt_post=nt, tm_post=tm, seq=seq)
        h = _ffn(h, ffn_norm[layer], ffn_w_up[layer], ffn_conv_w[layer], ffn_conv_b[layer],
                 ffn_w_down[layer], final_norm, nb=nb, nt=seq // tm_ffn, tm=tm_ffn,
                 final_norm=(layer == depth - 1))
    return h.reshape(nb, seq, d)
```

```python
import functools

import jax
import jax.numpy as jnp
from jax import lax
from jax.experimental import pallas as pl
from jax.experimental.pallas import tpu as pltpu

F32 = jnp.float32
BF16 = jnp.bfloat16

RMS_EPS = 1e-6
GN_EPS = 64e-5
LRU_C = 8.0
HEAD = 64
PAIR = 2 * HEAD
GATE_TILE = 256
SCAN_L = 32
PRE_SUB = 128
SOLVE_ROUNDS = 5
SUBLANES = 8
LANES = 128
BF16_ROWS = 16
SCAN_GROUP = 32

VMEM_LIMIT = 56 * 1024 * 1024


def _rms(x, g):
    ms = jnp.mean(x * x, axis=-1, keepdims=True)
    return x * lax.rsqrt(ms + RMS_EPS) * g


def _gelu(x):
    return x * (0.5 * (1.0 + jnp.tanh(0.7978845608028654 * (x + 0.044715 * (x * x * x)))))


def _sigmoid(x):
    return 0.5 * jnp.tanh(0.5 * x) + 0.5


def _sqrt_nonneg(x):
    return jnp.where(x > 0.0, x * lax.rsqrt(x), 0.0)


def _softplus(x):
    return jnp.maximum(x, 0.0) + jnp.log1p(jnp.exp(-jnp.abs(x)))


def _dot(a, b):
    return jnp.dot(a, b, preferred_element_type=F32)


def _dot_nt(a, b):
    return lax.dot_general(a, b, (((1,), (1,)), ((), ())), preferred_element_type=F32)


def _const_spec(shape):
    nd = len(shape)
    return pl.BlockSpec(shape, lambda *_: (0,) * nd, pipeline_mode=pl.Buffered(1))


def _params():
    return pltpu.CompilerParams(dimension_semantics=("arbitrary", "arbitrary"),
                                vmem_limit_bytes=VMEM_LIMIT)


def _shifted(buf, halo, cur, tm, first):
    @pl.when(first)
    def _():
        halo[...] = jnp.zeros_like(halo)
    buf[0:SUBLANES, :] = halo[...]
    buf[SUBLANES:SUBLANES + tm, :] = cur
    halo[...] = cur[tm - SUBLANES:tm, :]


def _ffn_kernel(x_ref, g_ref, wup_ref, cw_ref, cb_ref, wdn_ref, fn_ref, o_ref,
                ubuf, halo, hid_buf, *, tm, fc, dff, final_norm):
    first = pl.program_id(1) == 0

    @pl.when(first)
    def _():
        halo[...] = jnp.zeros_like(halo)

    x = x_ref[...]
    h = _rms(x, g_ref[...]).astype(BF16)
    for f in range(dff // fc):
        lo = f * fc
        ug = _dot(h, wup_ref[:, lo:lo + fc])
        uv = _dot(h, wup_ref[:, dff + lo:dff + lo + fc])
        ubuf[0:SUBLANES, :] = halo[f]
        ubuf[SUBLANES:SUBLANES + tm, :] = ug
        halo[f] = ug[tm - SUBLANES:tm, :]
        gate = (cb_ref[:, lo:lo + fc]
                + cw_ref[2:3, lo:lo + fc] * ug
                + cw_ref[1:2, lo:lo + fc] * ubuf[SUBLANES - 1:SUBLANES - 1 + tm, :]
                + cw_ref[0:1, lo:lo + fc] * ubuf[SUBLANES - 2:SUBLANES - 2 + tm, :])
        hid_buf[:, lo:lo + fc] = (_gelu(gate) * uv).astype(BF16)
    y = x + _dot(hid_buf[...], wdn_ref[...])
    if final_norm:
        y = _rms(y, fn_ref[...])
    o_ref[...] = y


def _ffn(x2d, norm, w_up, conv_w, conv_b, w_down, final_g, *, nb, nt, tm, final_norm):
    m, d = x2d.shape
    dff = w_down.shape[0]
    fc = 256
    row = lambda b, c: (b * nt + c, 0)
    kern = functools.partial(_ffn_kernel, tm=tm, fc=fc, dff=dff, final_norm=final_norm)
    return pl.pallas_call(
        kern,
        grid=(nb, nt),
        in_specs=[
            pl.BlockSpec((tm, d), row),
            _const_spec((1, d)),
            _const_spec((d, 2 * dff)),
            _const_spec((3, dff)),
            _const_spec((1, dff)),
            _const_spec((dff, d)),
            _const_spec((1, d)),
        ],
        out_specs=pl.BlockSpec((tm, d), row),
        out_shape=jax.ShapeDtypeStruct((m, d), F32),
        scratch_shapes=[
            pltpu.VMEM((SUBLANES + tm, fc), F32),
            pltpu.VMEM((dff // fc, SUBLANES, fc), F32),
            pltpu.VMEM((tm, dff), BF16),
        ],
        compiler_params=_params(),
        name="conv_ffn",
    )(x2d, norm.reshape(1, d), w_up.astype(BF16), conv_w, conv_b.reshape(1, dff),
      w_down.astype(BF16), final_g.reshape(1, d))


def _lru_kernel(x_ref, g_ref, win_ref, bin_ref, cw_ref, cb_ref, wg_ref, bg_ref, lam_ref,
                wout_ref, bout_ref, o_ref, xbuf, halo, abuf, bbuf, hbuf, hcar, *, tm, d):
    first = pl.program_id(1) == 0

    @pl.when(first)
    def _():
        hcar[...] = jnp.zeros_like(hcar)

    x = x_ref[...]
    h = _rms(x, g_ref[...]).astype(BF16)
    ux = _dot(h, win_ref[:, d:2 * d]) + bin_ref[:, d:2 * d]
    _shifted(xbuf, halo, ux, tm, first)
    xr = (cb_ref[...]
          + cw_ref[3:4, :] * ux
          + cw_ref[2:3, :] * xbuf[SUBLANES - 1:SUBLANES - 1 + tm, :]
          + cw_ref[1:2, :] * xbuf[SUBLANES - 2:SUBLANES - 2 + tm, :]
          + cw_ref[0:1, :] * xbuf[SUBLANES - 3:SUBLANES - 3 + tm, :])

    sp = _softplus(-lam_ref[...])
    sub = lax.broadcasted_iota(jnp.int32, (tm // SUBLANES, SUBLANES, GATE_TILE), 1)
    for q in range(d // GATE_TILE):
        lo = q * GATE_TILE
        xq = xr[:, lo:lo + GATE_TILE]
        xqb = xq.astype(BF16)
        rg = _sigmoid(_dot(xqb, wg_ref[0, q]) + bg_ref[0:1, lo:lo + GATE_TILE])
        ig = _sigmoid(_dot(xqb, wg_ref[1, q]) + bg_ref[1:2, lo:lo + GATE_TILE])
        log_a = (-LRU_C) * rg * sp[:, lo:lo + GATE_TILE]
        a = jnp.exp(log_a)
        bt = (1.0 + a) * _sqrt_nonneg(-jnp.tanh(0.5 * log_a)) * (ig * xq)
        a3 = a.reshape(tm // SUBLANES, SUBLANES, GATE_TILE)
        b3 = bt.reshape(tm // SUBLANES, SUBLANES, GATE_TILE)
        for s in (1, 2, 4):
            keep = sub >= s
            b3 = a3 * jnp.where(keep, pltpu.roll(b3, s, 1), 0.0) + b3
            a3 = a3 * jnp.where(keep, pltpu.roll(a3, s, 1), 1.0)
        abuf[:, lo:lo + GATE_TILE] = a3.reshape(tm, GATE_TILE)
        bbuf[:, lo:lo + GATE_TILE] = b3.reshape(tm, GATE_TILE)

    def carry_group(i, hprev):
        r0 = pl.multiple_of(i * SUBLANES, SUBLANES)
        h8 = abuf[pl.ds(r0, SUBLANES), :] * hprev + bbuf[pl.ds(r0, SUBLANES), :]
        hbuf[pl.ds(r0, SUBLANES), :] = h8
        return h8[SUBLANES - 1:SUBLANES, :]

    hlast = lax.fori_loop(0, tm // SUBLANES, carry_group, hcar[0:1, :], unroll=8)
    hcar[0:1, :] = hlast

    uy = _dot(h, win_ref[:, 0:d]) + bin_ref[:, 0:d]
    out = (hbuf[...] * _gelu(uy)).astype(BF16)
    o_ref[...] = x + _dot(out, wout_ref[...]) + bout_ref[...]


def _lru(x2d, norm, w_in, b_in, conv_w, conv_b, gate_w, gate_b, lam, w_out, b_out, *, nb, nt, tm):
    m, d = x2d.shape
    nblk, bw = gate_w.shape[1], gate_w.shape[2]
    per = GATE_TILE // bw
    w5 = gate_w.reshape(2, nblk // per, per, bw, bw)
    eye = jnp.eye(per, dtype=gate_w.dtype)
    wg = jnp.einsum('gqmcd,mn->gqmcnd', w5, eye).reshape(2, nblk // per, GATE_TILE, GATE_TILE)
    row = lambda b, c: (b * nt + c, 0)
    kern = functools.partial(_lru_kernel, tm=tm, d=d)
    return pl.pallas_call(
        kern,
        grid=(nb, nt),
        in_specs=[
            pl.BlockSpec((tm, d), row),
            _const_spec((1, d)),
            _const_spec((d, 2 * d)),
            _const_spec((1, 2 * d)),
            _const_spec((4, d)),
            _const_spec((1, d)),
            _const_spec((2, d // GATE_TILE, GATE_TILE, GATE_TILE)),
            _const_spec((2, d)),
            _const_spec((1, d)),
            _const_spec((d, d)),
            _const_spec((1, d)),
        ],
        out_specs=pl.BlockSpec((tm, d), row),
        out_shape=jax.ShapeDtypeStruct((m, d), F32),
        scratch_shapes=[
            pltpu.VMEM((SUBLANES + tm, d), F32),
            pltpu.VMEM((SUBLANES, d), F32),
            pltpu.VMEM((tm, d), F32),
            pltpu.VMEM((tm, d), F32),
            pltpu.VMEM((tm, d), F32),
            pltpu.VMEM((SUBLANES, d), F32),
        ],
        compiler_params=_params(),
        name="rglru",
    )(x2d, norm.reshape(1, d), w_in.astype(BF16), b_in.reshape(1, 2 * d), conv_w,
      conv_b.reshape(1, d), wg.astype(BF16), gate_b.reshape(2, d), lam.reshape(1, d),
      w_out.astype(BF16), b_out.reshape(1, d))


def _head_sum(x, jmat):
    hi = x.astype(BF16)
    lo = (x - hi.astype(F32)).astype(BF16)
    return _dot(hi, jmat) + _dot(lo, jmat)


def _pre_kernel(x_ref, g_ref, mix_ref, wrkv_ref, w0_ref, w1_ref, w2_ref, a0_ref, a1_ref, a2_ref,
                g1_ref, g2_ref, kk_ref, ka_ref, rk_ref, jm_ref,
                r_out, c_out, k_out, v_out, kn_out, b_out, g_out, bonus_out,
                hbuf, halo, *, tm, sub, d):
    @pl.when(pl.program_id(1) == 0)
    def _():
        halo[...] = jnp.zeros_like(halo)

    hbuf[0:SUBLANES, :] = halo[...]
    jm = jm_ref[...]
    tin = lax.broadcasted_iota(jnp.int32, (sub, d), 0) & (SCAN_L - 1)
    def front(r0):
        h = _rms(x_ref[r0:r0 + sub, :], g_ref[...])
        hbuf[SUBLANES + r0:SUBLANES + r0 + sub, :] = h
        if r0 + sub == tm:
            halo[...] = h[sub - SUBLANES:sub, :]
        xx = hbuf[SUBLANES - 1 + r0:SUBLANES - 1 + r0 + sub, :] - h

        hb = h.astype(BF16).reshape(sub // BF16_ROWS, BF16_ROWS, d)
        xxb = xx.astype(BF16).reshape(sub // BF16_ROWS, BF16_ROWS, d)

        def mixed(i):
            return (hb + xxb * mix_ref[i]).reshape(sub, d)

        w_lo = _dot(mixed(3), w1_ref[...])
        a_lo = _dot(mixed(4), a1_ref[...])
        g_lo = _dot(mixed(5), g1_ref[...])
        r = _dot(mixed(0), wrkv_ref[0])
        k = _dot(mixed(1), wrkv_ref[1])
        v = _dot(mixed(2), wrkv_ref[2])
        wl = _dot(jnp.tanh(w_lo).astype(BF16), w2_ref[...])
        al = _dot(a_lo.astype(BF16), a2_ref[...])
        g_out[r0:r0 + sub, :] = _dot(_sigmoid(g_lo).astype(BF16), g2_ref[...])
        return r0, r, k, v, wl, al

    def tail(piece):
        r0, r, k, v, wl, al = piece
        rows = slice(r0, r0 + sub)
        z = -(w0_ref[...] + wl)
        w = -(jnp.maximum(z, 0.0) + jnp.log(1.0 + jnp.exp(-jnp.abs(z)))) - 0.5
        lw = -jnp.exp(w)
        a = _sigmoid(a0_ref[...] + al)

        c = lw
        s = 1
        while s < SCAN_L:
            c = jnp.where(tin >= s, c + pltpu.roll(c, s, 0), c)
            s *= 2

        kf = k * (1.0 + (a - 1.0) * ka_ref[...])
        kk = k * kk_ref[...]
        rkk = r * kf * rk_ref[...]
        for p in range(d // PAIR):
            sl = slice(p * PAIR, (p + 1) * PAIR)
            kkp = kk[:, sl]
            kn = kkp * jnp.minimum(lax.rsqrt(_head_sum(kkp * kkp, jm)), 1e12)
            kn_out[p, rows, :] = kn
            b_out[p, rows, :] = kn * a[:, sl]
            bonus_out[rows, sl] = _head_sum(rkk[:, sl], jm) * v[:, sl]
            r_out[p, rows, :] = r[:, sl]
            c_out[p, rows, :] = c[:, sl]
            k_out[p, rows, :] = kf[:, sl]
            v_out[p, rows, :] = v[:, sl]

    pending = None
    for r0 in range(0, tm, sub):
        piece = front(r0)
        if pending is not None:
            tail(pending)
        pending = piece
    tail(pending)


def _scan_kernel(r_ref, c_ref, k_ref, v_ref, kn_ref, b_ref, y_ref, s_ref, *, nu):
    L = SCAN_L

    @pl.when(pl.program_id(1) == 0)
    def _():
        s_ref[...] = jnp.zeros_like(s_ref)

    lane = lax.broadcasted_iota(jnp.int32, (L, PAIR), 1)
    trow = lax.broadcasted_iota(jnp.int32, (L, PAIR), 0)
    h0 = lane < HEAD
    col_t = lane & (L - 1)
    incl = col_t <= trow
    srow = lax.broadcasted_iota(jnp.int32, (PAIR, PAIR), 0)
    scol = lax.broadcasted_iota(jnp.int32, (PAIR, PAIR), 1)
    same_head = (srow < HEAD) == (scol < HEAD)
    col2 = lax.broadcasted_iota(jnp.int32, (L, 2 * L), 1)
    trow2 = lax.broadcasted_iota(jnp.int32, (L, 2 * L), 0)
    strict2 = (col2 & (L - 1)) < trow2
    strict = col_t < trow
    nb0 = strict & (lane < L)
    nb1 = strict & (lane >= L) & (lane < 2 * L)
    h0z = lax.broadcasted_iota(jnp.int32, (2 * L, PAIR), 1) < HEAD

    def split(t):
        return jnp.concatenate([jnp.where(h0, t, 0.0), jnp.where(h0, 0.0, t)], axis=0)

    def group(gi, carry):
        units = [gi * SCAN_GROUP + j for j in range(SCAN_GROUP)]
        lhs, rhs, vm, vs, bk, gl = [], [], [], [], [], []
        for u in units:
            r = r_ref[u]
            c = c_ref[u]
            k = k_ref[u]
            v = v_ref[u]
            kn = kn_ref[u]
            bv = b_ref[u]
            c_last = c[L - 1:L, :]
            c_excl = jnp.where(trow >= 1, pltpu.roll(c, 1, 0), 0.0)
            e_neg = jnp.exp(-c)
            e_end = jnp.exp(c_last - c)
            rt = r * jnp.exp(c)
            at = -(kn * jnp.exp(c_excl))
            lhs.append(jnp.concatenate([at, rt], axis=0).astype(BF16))
            rhs.append(jnp.concatenate([split(bv * e_neg), split(k * e_neg)], axis=0).astype(BF16))
            vm.append(split(v))
            vs.append(v)
            bk.append(jnp.concatenate([bv * e_end, k * e_end], axis=0).astype(BF16))
            gl.append(jnp.exp(c_last))
        st = [s_ref[u] for u in units]
        ps = [_dot_nt(a, jnp.concatenate([b, s.astype(BF16)], axis=0)) for a, b, s in zip(lhs, rhs, st)]
        pbot = [jnp.where(incl, q[L:2 * L, 0:PAIR], 0.0) for q in ps]
        wak = [_dot(jnp.where(strict2, q[0:L, 2 * L:4 * L], 0.0).astype(BF16), m.astype(BF16))
               for q, m in zip(ps, vm)]
        w = [q[0:L, PAIR:2 * PAIR] + a for q, a in zip(ps, wak)]
        z = [jnp.concatenate(
            [jnp.where(h0, jnp.where(nb0, q[0:L, 0:PAIR], 0.0), pltpu.roll(ww, HEAD, 1)),
             jnp.where(h0, jnp.where(nb1, q[0:L, 0:PAIR], 0.0), ww)], axis=0)
             for q, ww in zip(ps, w)]
        for _ in range(SOLVE_ROUNDS):
            out = [_dot(zz[:, 0:2 * L].astype(BF16), zz.astype(BF16)) for zz in z]
            z = [o + jnp.where(h0z, 0.0, zz) for o, zz in zip(out, z)]
        uu = [jnp.where(h0, pltpu.roll(zz[0:L], HEAD, 1), zz[L:2 * L]) for zz in z]
        for u, q, pb, xx, m in zip(units, ps, pbot, uu, vm):
            y_ref[u] = q[L:2 * L, PAIR:2 * PAIR] + _dot(
                pb.astype(BF16), jnp.concatenate([split(xx), m], axis=0).astype(BF16))
        for u, s, xx, v, b, g in zip(units, st, uu, vs, bk, gl):
            uv_t = jnp.concatenate([xx, v], axis=0).T.astype(BF16)
            s_ref[u] = jnp.where(same_head, s * g + _dot(uv_t, b), 0.0)
        return carry

    lax.fori_loop(0, nu // SCAN_GROUP, group, 0)


def _post_kernel(y_ref, bonus_ref, g_ref, x_ref, lnw_ref, lnb_ref, wout_ref, jm_ref, o_ref,
                 zbuf, *, d):
    jm = jm_ref[...]
    for p in range(d // PAIR):
        sl = slice(p * PAIR, (p + 1) * PAIR)
        y = y_ref[p]
        mu = _head_sum(y, jm) * (1.0 / HEAD)
        dy = y - mu
        var = _head_sum(dy * dy, jm) * (1.0 / HEAD)
        yn = dy * lax.rsqrt(var + GN_EPS) * lnw_ref[:, sl] + lnb_ref[:, sl]
        zbuf[:, sl] = ((yn + bonus_ref[:, sl]) * g_ref[:, sl]).astype(BF16)
    o_ref[...] = x_ref[...] + _dot(zbuf[...], wout_ref[...])


def _rwkv(x2d, norm, mix, w_rkv, w0, w1, w2, a0, a1, a2, g1, g2, k_k, k_a, r_k, ln_w, ln_b, w_out,
          *, nb, nt_pre, tm_pre, nt_post, tm_post, seq):
    m, d = x2d.shape
    npair = d // PAIR
    assert 4 * SCAN_L == PAIR and tm_pre % SCAN_L == 0
    idx = jnp.arange(PAIR) // HEAD
    jm = (idx[:, None] == idx[None, :]).astype(BF16)
    vec = lambda t: t.reshape(1, d)
    row_pre = lambda b, c: (b * nt_pre + c, 0)
    pair_pre = lambda b, c: (b, c, 0)
    pair_shape = jax.ShapeDtypeStruct((nb * npair, seq, PAIR), F32)
    pair_spec = pl.BlockSpec((npair, tm_pre, PAIR), pair_pre)
    flat_shape = jax.ShapeDtypeStruct((m, d), F32)
    r, c, k, v, kn, bv, g, bonus = pl.pallas_call(
        functools.partial(_pre_kernel, tm=tm_pre, sub=min(tm_pre, PRE_SUB), d=d),
        grid=(nb, nt_pre),
        in_specs=[
            pl.BlockSpec((tm_pre, d), row_pre),
            _const_spec((1, d)),
            _const_spec((6, BF16_ROWS, d)),
            _const_spec((3, d, d)),
            _const_spec((1, d)),
            _const_spec(w1.shape),
            _const_spec(w2.shape),
            _const_spec((1, d)),
            _const_spec(a1.shape),
            _const_spec(a2.shape),
            _const_spec(g1.shape),
            _const_spec(g2.shape),
            _const_spec((1, d)),
            _const_spec((1, d)),
            _const_spec((1, d)),
            _const_spec((PAIR, PAIR)),
        ],
        out_specs=[pair_spec] * 6 + [pl.BlockSpec((tm_pre, d), row_pre)] * 2,
        out_shape=[pair_shape] * 6 + [flat_shape] * 2,
        scratch_shapes=[
            pltpu.VMEM((SUBLANES + tm_pre, d), F32),
            pltpu.VMEM((SUBLANES, d), F32),
        ],
        compiler_params=_params(),
        name="rwkv_pre",
    )(x2d, vec(norm), jnp.broadcast_to(mix[:, None, :], (6, BF16_ROWS, d)).astype(BF16),
      w_rkv.astype(BF16), vec(w0), w1.astype(BF16), w2.astype(BF16),
      vec(a0), a1.astype(BF16), a2.astype(BF16), g1.astype(BF16), g2.astype(BF16),
      vec(k_k), vec(k_a), vec(r_k), jm)

    nu = 4 * npair
    unit_spec = pl.BlockSpec((nu, SCAN_L, PAIR), lambda ug, c: (ug, c, 0))
    y = pl.pallas_call(
        functools.partial(_scan_kernel, nu=nu),
        grid=(nb * npair // nu, seq // SCAN_L),
        in_specs=[unit_spec] * 6,
        out_specs=unit_spec,
        out_shape=pair_shape,
        scratch_shapes=[pltpu.VMEM((nu, PAIR, PAIR), F32)],
        compiler_params=_params(),
        name="rwkv_scan",
    )(r, c, k, v, kn, bv)

    row_post = lambda b, c: (b * nt_post + c, 0)
    return pl.pallas_call(
        functools.partial(_post_kernel, d=d),
        grid=(nb, nt_post),
        in_specs=[
            pl.BlockSpec((npair, tm_post, PAIR), lambda b, c: (b, c, 0)),
            pl.BlockSpec((tm_post, d), row_post),
            pl.BlockSpec((tm_post, d), row_post),
            pl.BlockSpec((tm_post, d), row_post),
            _const_spec((1, d)),
            _const_spec((1, d)),
            _const_spec((d, d)),
            _const_spec((PAIR, PAIR)),
        ],
        out_specs=pl.BlockSpec((tm_post, d), row_post),
        out_shape=flat_shape,
        scratch_shapes=[pltpu.VMEM((tm_post, d), BF16)],
        compiler_params=_params(),
        name="rwkv_post",
    )(y, bonus, g, x2d, vec(ln_w), vec(ln_b), w_out.astype(BF16), jm)


def _tile(seq, want):
    return want if seq % want == 0 else seq


def kernel(x, lru_norm, lru_w_in, lru_b_in, lru_conv_w, lru_conv_b, lru_gate_w, lru_gate_b, lru_lambda, lru_w_out, lru_b_out, rwkv_norm, rwkv_mix, rwkv_w_rkv, rwkv_w0, rwkv_w1, rwkv_w2, rwkv_a0, rwkv_a1, rwkv_a2, rwkv_g1, rwkv_g2, rwkv_k_k, rwkv_k_a, rwkv_r_k, rwkv_ln_w, rwkv_ln_b, rwkv_w_out, ffn_norm, ffn_w_up, ffn_conv_w, ffn_conv_b, ffn_w_down, final_norm):
    nb, seq, d = x.shape
    depth = ffn_norm.shape[0]
    tm = _tile(seq, 512)
    tm_pre = _tile(seq, 512)
    tm_ffn = _tile(seq, 1024)
    nt = seq // tm
    h = x.reshape(nb * seq, d)
    for layer in range(depth):
        j = layer // 2
        if layer % 2 == 0:
            h = _lru(h, lru_norm[j], lru_w_in[j], lru_b_in[j], lru_conv_w[j], lru_conv_b[j],
                     lru_gate_w[j], lru_gate_b[j], lru_lambda[j], lru_w_out[j], lru_b_out[j],
                     nb=nb, nt=nt, tm=tm)
        else:
            h = _rwkv(h, rwkv_norm[j], rwkv_mix[j], rwkv_w_rkv[j], rwkv_w0[j], rwkv_w1[j],
                      rwkv_w2[j], rwkv_a0[j], rwkv_a1[j], rwkv_a2[j], rwkv_g1[j], rwkv_g2[j],
                      rwkv_k_k[j], rwkv_k_a[j], rwkv_r_k[j], rwkv_ln_w[j], rwkv_ln_b[j],
                      rwkv_w_out[j], nb=nb, nt_pre=seq // tm_pre, tm_pre=tm_pre,
                      nt_post=nt, tm_post=tm, seq=seq)
        h = _ffn(h, ffn_norm[layer], ffn_w_up[layer], ffn_conv_w[layer], ffn_conv_b[layer],
                 ffn_w_down[layer], final_norm, nb=nb, nt=seq // tm_ffn, tm=tm_ffn,
                 final_norm=(layer == depth - 1))
    return h.reshape(nb, seq, d)
```

```python
import functools

import jax
import jax.numpy as jnp
from jax import lax
from jax.experimental import pallas as pl
from jax.experimental.pallas import tpu as pltpu

F32 = jnp.float32
BF16 = jnp.bfloat16

RMS_EPS = 1e-6
GN_EPS = 64e-5
LRU_C = 8.0
HEAD = 64
PAIR = 2 * HEAD
GATE_TILE = 256
SCAN_L = 32
PRE_SUB = 128
SOLVE_ROUNDS = 5
SUBLANES = 8
LANES = 128
BF16_ROWS = 16
SCAN_GROUP = 32

VMEM_LIMIT = 56 * 1024 * 1024


def _rms(x, g):
    ms = jnp.mean(x * x, axis=-1, keepdims=True)
    return x * lax.rsqrt(ms + RMS_EPS) * g


def _gelu(x):
    return x * (0.5 * (1.0 + jnp.tanh(0.7978845608028654 * (x + 0.044715 * (x * x * x)))))


def _sigmoid(x):
    return 0.5 * jnp.tanh(0.5 * x) + 0.5


def _sqrt_nonneg(x):
    return jnp.where(x > 0.0, x * lax.rsqrt(x), 0.0)


def _softplus(x):
    return jnp.maximum(x, 0.0) + jnp.log1p(jnp.exp(-jnp.abs(x)))


def _dot(a, b):
    return jnp.dot(a, b, preferred_element_type=F32)


def _dot_nt(a, b):
    return lax.dot_general(a, b, (((1,), (1,)), ((), ())), preferred_element_type=F32)


def _const_spec(shape):
    nd = len(shape)
    return pl.BlockSpec(shape, lambda *_: (0,) * nd, pipeline_mode=pl.Buffered(1))


def _params():
    return pltpu.CompilerParams(dimension_semantics=("arbitrary", "arbitrary"),
                                vmem_limit_bytes=VMEM_LIMIT)


def _shifted(buf, halo, cur, tm, first):
    @pl.when(first)
    def _():
        halo[...] = jnp.zeros_like(halo)
    buf[0:SUBLANES, :] = halo[...]
    buf[SUBLANES:SUBLANES + tm, :] = cur
    halo[...] = cur[tm - SUBLANES:tm, :]


def _ffn_kernel(x_ref, g_ref, wup_ref, cw_ref, cb_ref, wdn_ref, fn_ref, o_ref,
                ubuf, halo, hid_buf, *, tm, fc, dff, final_norm):
    first = pl.program_id(1) == 0

    @pl.when(first)
    def _():
        halo[...] = jnp.zeros_like(halo)

    x = x_ref[...]
    h = _rms(x, g_ref[...]).astype(BF16)
    for f in range(dff // fc):
        lo = f * fc
        ug = _dot(h, wup_ref[:, lo:lo + fc])
        uv = _dot(h, wup_ref[:, dff + lo:dff + lo + fc])
        ubuf[0:SUBLANES, :] = halo[f]
        ubuf[SUBLANES:SUBLANES + tm, :] = ug
        halo[f] = ug[tm - SUBLANES:tm, :]
        gate = (cb_ref[:, lo:lo + fc]
                + cw_ref[2:3, lo:lo + fc] * ug
                + cw_ref[1:2, lo:lo + fc] * ubuf[SUBLANES - 1:SUBLANES - 1 + tm, :]
                + cw_ref[0:1, lo:lo + fc] * ubuf[SUBLANES - 2:SUBLANES - 2 + tm, :])
        hid_buf[:, lo:lo + fc] = (_gelu(gate) * uv).astype(BF16)
    y = x + _dot(hid_buf[...], wdn_ref[...])
    if final_norm:
        y = _rms(y, fn_ref[...])
    o_ref[...] = y


def _ffn(x2d, norm, w_up, conv_w, conv_b, w_down, final_g, *, nb, nt, tm, final_norm):
    m, d = x2d.shape
    dff = w_down.shape[0]
    fc = 256
    row = lambda b, c: (b * nt + c, 0)
    kern = functools.partial(_ffn_kernel, tm=tm, fc=fc, dff=dff, final_norm=final_norm)
    return pl.pallas_call(
        kern,
        grid=(nb, nt),
        in_specs=[
            pl.BlockSpec((tm, d), row),
            _const_spec((1, d)),
            _const_spec((d, 2 * dff)),
            _const_spec((3, dff)),
            _const_spec((1, dff)),
            _const_spec((dff, d)),
            _const_spec((1, d)),
        ],
        out_specs=pl.BlockSpec((tm, d), row),
        out_shape=jax.ShapeDtypeStruct((m, d), F32),
        scratch_shapes=[
            pltpu.VMEM((SUBLANES + tm, fc), F32),
            pltpu.VMEM((dff // fc, SUBLANES, fc), F32),
            pltpu.VMEM((tm, dff), BF16),
        ],
        compiler_params=_params(),
        name="conv_ffn",
    )(x2d, norm.reshape(1, d), w_up.astype(BF16), conv_w, conv_b.reshape(1, dff),
      w_down.astype(BF16), final_g.reshape(1, d))


def _lru_kernel(x_ref, g_ref, win_ref, bin_ref, cw_ref, cb_ref, wg_ref, bg_ref, lam_ref,
                wout_ref, bout_ref, o_ref, xbuf, halo, obuf, hcar, *, tm, d):
    first = pl.program_id(1) == 0

    @pl.when(first)
    def _():
        hcar[...] = jnp.zeros_like(hcar)

    x = x_ref[...]
    h = _rms(x, g_ref[...]).astype(BF16)
    ux = _dot(h, win_ref[:, d:2 * d]) + bin_ref[:, d:2 * d]
    uy = _dot(h, win_ref[:, 0:d]) + bin_ref[:, 0:d]
    _shifted(xbuf, halo, ux, tm, first)
    xr = (cb_ref[...]
          + cw_ref[3:4, :] * ux
          + cw_ref[2:3, :] * xbuf[SUBLANES - 1:SUBLANES - 1 + tm, :]
          + cw_ref[1:2, :] * xbuf[SUBLANES - 2:SUBLANES - 2 + tm, :]
          + cw_ref[0:1, :] * xbuf[SUBLANES - 3:SUBLANES - 3 + tm, :])

    sp = _softplus(-lam_ref[...])
    ngrp = tm // SUBLANES
    sub = lax.broadcasted_iota(jnp.int32, (ngrp, SUBLANES, GATE_TILE), 1)
    for q in range(d // GATE_TILE):
        lo = q * GATE_TILE
        xq = xr[:, lo:lo + GATE_TILE]
        xqb = xq.astype(BF16)
        rg = _sigmoid(_dot(xqb, wg_ref[0, q]) + bg_ref[0:1, lo:lo + GATE_TILE])
        ig = _sigmoid(_dot(xqb, wg_ref[1, q]) + bg_ref[1:2, lo:lo + GATE_TILE])
        log_a = (-LRU_C) * rg * sp[:, lo:lo + GATE_TILE]
        a = jnp.exp(log_a)
        bt = (1.0 + a) * _sqrt_nonneg(-jnp.tanh(0.5 * log_a)) * (ig * xq)
        a3 = a.reshape(ngrp, SUBLANES, GATE_TILE)
        b3 = bt.reshape(ngrp, SUBLANES, GATE_TILE)
        for s in (1, 2, 4):
            keep = sub >= s
            b3 = a3 * jnp.where(keep, pltpu.roll(b3, s, 1), 0.0) + b3
            a3 = a3 * jnp.where(keep, pltpu.roll(a3, s, 1), 1.0)
        hprev = hcar[0:1, lo:lo + GATE_TILE]
        groups = []
        for i in range(ngrp):
            h8 = a3[i] * hprev + b3[i]
            groups.append(h8)
            hprev = h8[SUBLANES - 1:SUBLANES, :]
        hcar[0:1, lo:lo + GATE_TILE] = hprev
        hq = jnp.concatenate(groups, axis=0)
        obuf[:, lo:lo + GATE_TILE] = (hq * _gelu(uy[:, lo:lo + GATE_TILE])).astype(BF16)

    o_ref[...] = x + _dot(obuf[...], wout_ref[...]) + bout_ref[...]


def _lru(x2d, norm, w_in, b_in, conv_w, conv_b, gate_w, gate_b, lam, w_out, b_out, *, nb, nt, tm):
    m, d = x2d.shape
    nblk, bw = gate_w.shape[1], gate_w.shape[2]
    per = GATE_TILE // bw
    w5 = gate_w.reshape(2, nblk // per, per, bw, bw)
    eye = jnp.eye(per, dtype=gate_w.dtype)
    wg = jnp.einsum('gqmcd,mn->gqmcnd', w5, eye).reshape(2, nblk // per, GATE_TILE, GATE_TILE)
    row = lambda b, c: (b * nt + c, 0)
    kern = functools.partial(_lru_kernel, tm=tm, d=d)
    return pl.pallas_call(
        kern,
        grid=(nb, nt),
        in_specs=[
            pl.BlockSpec((tm, d), row),
            _const_spec((1, d)),
            _const_spec((d, 2 * d)),
            _const_spec((1, 2 * d)),
            _const_spec((4, d)),
            _const_spec((1, d)),
            _const_spec((2, d // GATE_TILE, GATE_TILE, GATE_TILE)),
            _const_spec((2, d)),
            _const_spec((1, d)),
            _const_spec((d, d)),
            _const_spec((1, d)),
        ],
        out_specs=pl.BlockSpec((tm, d), row),
        out_shape=jax.ShapeDtypeStruct((m, d), F32),
        scratch_shapes=[
            pltpu.VMEM((SUBLANES + tm, d), F32),
            pltpu.VMEM((SUBLANES, d), F32),
            pltpu.VMEM((tm, d), BF16),
            pltpu.VMEM((SUBLANES, d), F32),
        ],
        compiler_params=_params(),
        name="rglru",
    )(x2d, norm.reshape(1, d), w_in.astype(BF16), b_in.reshape(1, 2 * d), conv_w,
      conv_b.reshape(1, d), wg.astype(BF16), gate_b.reshape(2, d), lam.reshape(1, d),
      w_out.astype(BF16), b_out.reshape(1, d))


def _head_sum(x, jmat):
    hi = x.astype(BF16)
    lo = (x - hi.astype(F32)).astype(BF16)
    return _dot(hi, jmat) + _dot(lo, jmat)


def _pre_kernel(x_ref, g_ref, mix_ref, wrkv_ref, w0_ref, w1_ref, w2_ref, a0_ref, a1_ref, a2_ref,
                g1_ref, g2_ref, kk_ref, ka_ref, rk_ref, jm_ref,
                r_out, c_out, k_out, v_out, kn_out, b_out, g_out, bonus_out,
                hbuf, halo, *, tm, sub, d):
    @pl.when(pl.program_id(1) == 0)
    def _():
        halo[...] = jnp.zeros_like(halo)

    hbuf[0:SUBLANES, :] = halo[...]
    jm = jm_ref[...]
    tin = lax.broadcasted_iota(jnp.int32, (sub, d), 0) & (SCAN_L - 1)
    def front(r0):
        h = _rms(x_ref[r0:r0 + sub, :], g_ref[...])
        hbuf[SUBLANES + r0:SUBLANES + r0 + sub, :] = h
        if r0 + sub == tm:
            halo[...] = h[sub - SUBLANES:sub, :]
        xx = hbuf[SUBLANES - 1 + r0:SUBLANES - 1 + r0 + sub, :] - h

        hb = h.astype(BF16).reshape(sub // BF16_ROWS, BF16_ROWS, d)
        xxb = xx.astype(BF16).reshape(sub // BF16_ROWS, BF16_ROWS, d)

        def mixed(i):
            return (hb + xxb * mix_ref[i]).reshape(sub, d)

        w_lo = _dot(mixed(3), w1_ref[...])
        a_lo = _dot(mixed(4), a1_ref[...])
        g_lo = _dot(mixed(5), g1_ref[...])
        r = _dot(mixed(0), wrkv_ref[0])
        k = _dot(mixed(1), wrkv_ref[1])
        v = _dot(mixed(2), wrkv_ref[2])
        wl = _dot(jnp.tanh(w_lo).astype(BF16), w2_ref[...])
        al = _dot(a_lo.astype(BF16), a2_ref[...])
        g_out[r0:r0 + sub, :] = _dot(_sigmoid(g_lo).astype(BF16), g2_ref[...]).astype(BF16)
        return r0, r, k, v, wl, al

    def tail(piece):
        r0, r, k, v, wl, al = piece
        rows = slice(r0, r0 + sub)
        z = -(w0_ref[...] + wl)
        w = -(jnp.maximum(z, 0.0) + jnp.log(1.0 + jnp.exp(-jnp.abs(z)))) - 0.5
        lw = -jnp.exp(w)
        a = _sigmoid(a0_ref[...] + al)

        c = lw
        s = 1
        while s < SCAN_L:
            c = jnp.where(tin >= s, c + pltpu.roll(c, s, 0), c)
            s *= 2

        kf = k * (1.0 + (a - 1.0) * ka_ref[...])
        kk = k * kk_ref[...]
        rkk = r * kf * rk_ref[...]
        for p in range(d // PAIR):
            sl = slice(p * PAIR, (p + 1) * PAIR)
            kkp = kk[:, sl]
            kn = kkp * jnp.minimum(lax.rsqrt(_head_sum(kkp * kkp, jm)), 1e12)
            kn_out[p, rows, :] = kn
            b_out[p, rows, :] = kn * a[:, sl]
            bonus_out[rows, sl] = (_head_sum(rkk[:, sl], jm) * v[:, sl]).astype(BF16)
            r_out[p, rows, :] = r[:, sl]
            c_out[p, rows, :] = c[:, sl]
            k_out[p, rows, :] = kf[:, sl]
            v_out[p, rows, :] = v[:, sl]

    pending = None
    for r0 in range(0, tm, sub):
        piece = front(r0)
        if pending is not None:
            tail(pending)
        pending = piece
    tail(pending)


def _scan_kernel(r_ref, c_ref, k_ref, v_ref, kn_ref, b_ref, y_ref, s_ref, *, nu):
    L = SCAN_L

    @pl.when(pl.program_id(1) == 0)
    def _():
        s_ref[...] = jnp.zeros_like(s_ref)

    lane = lax.broadcasted_iota(jnp.int32, (L, PAIR), 1)
    trow = lax.broadcasted_iota(jnp.int32, (L, PAIR), 0)
    h0 = lane < HEAD
    col_t = lane & (L - 1)
    incl = col_t <= trow
    srow = lax.broadcasted_iota(jnp.int32, (PAIR, PAIR), 0)
    scol = lax.broadcasted_iota(jnp.int32, (PAIR, PAIR), 1)
    same_head = (srow < HEAD) == (scol < HEAD)
    col2 = lax.broadcasted_iota(jnp.int32, (L, 2 * L), 1)
    trow2 = lax.broadcasted_iota(jnp.int32, (L, 2 * L), 0)
    strict2 = (col2 & (L - 1)) < trow2
    strict = col_t < trow
    nb0 = strict & (lane < L)
    nb1 = strict & (lane >= L) & (lane < 2 * L)
    h0z = lax.broadcasted_iota(jnp.int32, (2 * L, PAIR), 1) < HEAD

    def split(t):
        return jnp.concatenate([jnp.where(h0, t, 0.0), jnp.where(h0, 0.0, t)], axis=0)

    def group(gi, carry):
        units = [gi * SCAN_GROUP + j for j in range(SCAN_GROUP)]
        lhs, rhs, vm, vs, bk, gl = [], [], [], [], [], []
        for u in units:
            r = r_ref[u]
            c = c_ref[u]
            k = k_ref[u]
            v = v_ref[u]
            kn = kn_ref[u]
            bv = b_ref[u]
            c_last = c[L - 1:L, :]
            c_excl = jnp.where(trow >= 1, pltpu.roll(c, 1, 0), 0.0)
            e_neg = jnp.exp(-c)
            e_end = jnp.exp(c_last - c)
            rt = r * jnp.exp(c)
            at = -(kn * jnp.exp(c_excl))
            lhs.append(jnp.concatenate([at, rt], axis=0).astype(BF16))
            rhs.append(jnp.concatenate([split(bv * e_neg), split(k * e_neg)], axis=0).astype(BF16))
            vm.append(split(v))
            vs.append(v)
            bk.append(jnp.concatenate([bv * e_end, k * e_end], axis=0).astype(BF16))
            gl.append(jnp.exp(c_last))
        st = [s_ref[u] for u in units]
        ps = [_dot_nt(a, jnp.concatenate([b, s.astype(BF16)], axis=0)) for a, b, s in zip(lhs, rhs, st)]
        pbot = [jnp.where(incl, q[L:2 * L, 0:PAIR], 0.0) for q in ps]
        wak = [_dot(jnp.where(strict2, q[0:L, 2 * L:4 * L], 0.0).astype(BF16), m.astype(BF16))
               for q, m in zip(ps, vm)]
        w = [q[0:L, PAIR:2 * PAIR] + a for q, a in zip(ps, wak)]
        z = [jnp.concatenate(
            [jnp.where(h0, jnp.where(nb0, q[0:L, 0:PAIR], 0.0), pltpu.roll(ww, HEAD, 1)),
             jnp.where(h0, jnp.where(nb1, q[0:L, 0:PAIR], 0.0), ww)], axis=0)
             for q, ww in zip(ps, w)]
        for _ in range(SOLVE_ROUNDS):
            out = [_dot(zz[:, 0:2 * L].astype(BF16), zz.astype(BF16)) for zz in z]
            z = [o + jnp.where(h0z, 0.0, zz) for o, zz in zip(out, z)]
        uu = [jnp.where(h0, pltpu.roll(zz[0:L], HEAD, 1), zz[L:2 * L]) for zz in z]
        for u, q, pb, xx, m in zip(units, ps, pbot, uu, vm):
            y_ref[u] = q[L:2 * L, PAIR:2 * PAIR] + _dot(
                pb.astype(BF16), jnp.concatenate([split(xx), m], axis=0).astype(BF16))
        for u, s, xx, v, b, g in zip(units, st, uu, vs, bk, gl):
            uv_t = jnp.concatenate([xx, v], axis=0).T.astype(BF16)
            s_ref[u] = jnp.where(same_head, s * g + _dot(uv_t, b), 0.0)
        return carry

    lax.fori_loop(0, nu // SCAN_GROUP, group, 0)


def _post_kernel(y_ref, bonus_ref, g_ref, x_ref, lnw_ref, lnb_ref, wout_ref, jm_ref, o_ref,
                 zbuf, *, d):
    jm = jm_ref[...]
    for p in range(d // PAIR):
        sl = slice(p * PAIR, (p + 1) * PAIR)
        y = y_ref[p]
        mu = _head_sum(y, jm) * (1.0 / HEAD)
        dy = y - mu
        var = _head_sum(dy * dy, jm) * (1.0 / HEAD)
        yn = dy * lax.rsqrt(var + GN_EPS) * lnw_ref[:, sl] + lnb_ref[:, sl]
        zbuf[:, sl] = ((yn + bonus_ref[:, sl].astype(F32)) * g_ref[:, sl].astype(F32)).astype(BF16)
    o_ref[...] = x_ref[...] + _dot(zbuf[...], wout_ref[...])


def _rwkv(x2d, norm, mix, w_rkv, w0, w1, w2, a0, a1, a2, g1, g2, k_k, k_a, r_k, ln_w, ln_b, w_out,
          *, nb, nt_pre, tm_pre, nt_post, tm_post, seq):
    m, d = x2d.shape
    npair = d // PAIR
    assert 4 * SCAN_L == PAIR and tm_pre % SCAN_L == 0
    idx = jnp.arange(PAIR) // HEAD
    jm = (idx[:, None] == idx[None, :]).astype(BF16)
    vec = lambda t: t.reshape(1, d)
    row_pre = lambda b, c: (b * nt_pre + c, 0)
    pair_pre = lambda b, c: (b, c, 0)
    pair_shape = jax.ShapeDtypeStruct((nb * npair, seq, PAIR), F32)
    pair_spec = pl.BlockSpec((npair, tm_pre, PAIR), pair_pre)
    flat_shape = jax.ShapeDtypeStruct((m, d), F32)
    r, c, k, v, kn, bv, g, bonus = pl.pallas_call(
        functools.partial(_pre_kernel, tm=tm_pre, sub=min(tm_pre, PRE_SUB), d=d),
        grid=(nb, nt_pre),
        in_specs=[
            pl.BlockSpec((tm_pre, d), row_pre),
            _const_spec((1, d)),
            _const_spec((6, BF16_ROWS, d)),
            _const_spec((3, d, d)),
            _const_spec((1, d)),
            _const_spec(w1.shape),
            _const_spec(w2.shape),
            _const_spec((1, d)),
            _const_spec(a1.shape),
            _const_spec(a2.shape),
            _const_spec(g1.shape),
            _const_spec(g2.shape),
            _const_spec((1, d)),
            _const_spec((1, d)),
            _const_spec((1, d)),
            _const_spec((PAIR, PAIR)),
        ],
        out_specs=[pair_spec] * 6 + [pl.BlockSpec((tm_pre, d), row_pre)] * 2,
        out_shape=[pair_shape] * 6 + [jax.ShapeDtypeStruct((m, d), BF16)] * 2,
        scratch_shapes=[
            pltpu.VMEM((SUBLANES + tm_pre, d), F32),
            pltpu.VMEM((SUBLANES, d), F32),
        ],
        compiler_params=_params(),
        name="rwkv_pre",
    )(x2d, vec(norm), jnp.broadcast_to(mix[:, None, :], (6, BF16_ROWS, d)).astype(BF16),
      w_rkv.astype(BF16), vec(w0), w1.astype(BF16), w2.astype(BF16),
      vec(a0), a1.astype(BF16), a2.astype(BF16), g1.astype(BF16), g2.astype(BF16),
      vec(k_k), vec(k_a), vec(r_k), jm)

    nu = 4 * npair
    unit_spec = pl.BlockSpec((nu, SCAN_L, PAIR), lambda ug, c: (ug, c, 0))
    y = pl.pallas_call(
        functools.partial(_scan_kernel, nu=nu),
        grid=(nb * npair // nu, seq // SCAN_L),
        in_specs=[unit_spec] * 6,
        out_specs=unit_spec,
        out_shape=pair_shape,
        scratch_shapes=[pltpu.VMEM((nu, PAIR, PAIR), F32)],
        compiler_params=_params(),
        name="rwkv_scan",
    )(r, c, k, v, kn, bv)

    row_post = lambda b, c: (b * nt_post + c, 0)
    return pl.pallas_call(
        functools.partial(_post_kernel, d=d),
        grid=(nb, nt_post),
        in_specs=[
            pl.BlockSpec((npair, tm_post, PAIR), lambda b, c: (b, c, 0)),
            pl.BlockSpec((tm_post, d), row_post),
            pl.BlockSpec((tm_post, d), row_post),
            pl.BlockSpec((tm_post, d), row_post),
            _const_spec((1, d)),
            _const_spec((1, d)),
            _const_spec((d, d)),
            _const_spec((PAIR, PAIR)),
        ],
        out_specs=pl.BlockSpec((tm_post, d), row_post),
        out_shape=flat_shape,
        scratch_shapes=[pltpu.VMEM((tm_post, d), BF16)],
        compiler_params=_params(),
        name="rwkv_post",
    )(y, bonus, g, x2d, vec(ln_w), vec(ln_b), w_out.astype(BF16), jm)


def _tile(seq, want):
    return want if seq % want == 0 else seq


def kernel(x, lru_norm, lru_w_in, lru_b_in, lru_conv_w, lru_conv_b, lru_gate_w, lru_gate_b, lru_lambda, lru_w_out, lru_b_out, rwkv_norm, rwkv_mix, rwkv_w_rkv, rwkv_w0, rwkv_w1, rwkv_w2, rwkv_a0, rwkv_a1, rwkv_a2, rwkv_g1, rwkv_g2, rwkv_k_k, rwkv_k_a, rwkv_r_k, rwkv_ln_w, rwkv_ln_b, rwkv_w_out, ffn_norm, ffn_w_up, ffn_conv_w, ffn_conv_b, ffn_w_down, final_norm):
    nb, seq, d = x.shape
    depth = ffn_norm.shape[0]
    tm = _tile(seq, 512)
    tm_pre = _tile(seq, 512)
    tm_ffn = _tile(seq, 1024)
    nt = seq // tm
    h = x.reshape(nb * seq, d)
    for layer in range(depth):
        j = layer // 2
        if layer % 2 == 0:
            h = _lru(h, lru_norm[j], lru_w_in[j], lru_b_in[j], lru_conv_w[j], lru_conv_b[j],
                     lru_gate_w[j], lru_gate_b[j], lru_lambda[j], lru_w_out[j], lru_b_out[j],
                     nb=nb, nt=nt, tm=tm)
        else:
            h = _rwkv(h, rwkv_norm[j], rwkv_mix[j], rwkv_w_rkv[j], rwkv_w0[j], rwkv_w1[j],
                      rwkv_w2[j], rwkv_a0[j], rwkv_a1[j], rwkv_a2[j], rwkv_g1[j], rwkv_g2[j],
                      rwkv_k_k[j], rwkv_k_a[j], rwkv_r_k[j], rwkv_ln_w[j], rwkv_ln_b[j],
                      rwkv_w_out[j], nb=nb, nt_pre=seq // tm_pre, tm_pre=tm_pre,
                      nt_post=nt, tm_post=tm, seq=seq)
        h = _ffn(h, ffn_norm[layer], ffn_w_up[layer], ffn_conv_w[layer], ffn_conv_b[layer],
                 ffn_w_down[layer], final_norm, nb=nb, nt=seq // tm_ffn, tm=tm_ffn,
                 final_norm=(layer == depth - 1))
    return h.reshape(nb, seq, d)
```

```python
import functools

import jax
import jax.numpy as jnp
from jax import lax
from jax.experimental import pallas as pl
from jax.experimental.pallas import tpu as pltpu

F32 = jnp.float32
BF16 = jnp.bfloat16

RMS_EPS = 1e-6
GN_EPS = 64e-5
LRU_C = 8.0
HEAD = 64
PAIR = 2 * HEAD
GATE_TILE = 256
SCAN_L = 32
PRE_SUB = 128
SOLVE_ROUNDS = 5
SUBLANES = 8
LANES = 128
BF16_ROWS = 16
SCAN_GROUP = 32

VMEM_LIMIT = 56 * 1024 * 1024


def _rms(x, g):
    ms = jnp.mean(x * x, axis=-1, keepdims=True)
    return x * lax.rsqrt(ms + RMS_EPS) * g


def _gelu(x):
    return x * (0.5 * (1.0 + jnp.tanh(0.7978845608028654 * (x + 0.044715 * (x * x * x)))))


def _sigmoid(x):
    return 0.5 * jnp.tanh(0.5 * x) + 0.5


def _sqrt_nonneg(x):
    return jnp.where(x > 0.0, x * lax.rsqrt(x), 0.0)


def _softplus(x):
    return jnp.maximum(x, 0.0) + jnp.log1p(jnp.exp(-jnp.abs(x)))


def _dot(a, b):
    return jnp.dot(a, b, preferred_element_type=F32)


def _dot_nt(a, b):
    return lax.dot_general(a, b, (((1,), (1,)), ((), ())), preferred_element_type=F32)


def _const_spec(shape):
    nd = len(shape)
    return pl.BlockSpec(shape, lambda *_: (0,) * nd, pipeline_mode=pl.Buffered(1))


def _params():
    return pltpu.CompilerParams(dimension_semantics=("arbitrary", "arbitrary"),
                                vmem_limit_bytes=VMEM_LIMIT)


def _shifted(buf, halo, cur, tm, first):
    @pl.when(first)
    def _():
        halo[...] = jnp.zeros_like(halo)
    buf[0:SUBLANES, :] = halo[...]
    buf[SUBLANES:SUBLANES + tm, :] = cur
    halo[...] = cur[tm - SUBLANES:tm, :]


def _ffn_kernel(x_ref, g_ref, wup_ref, cw_ref, cb_ref, wdn_ref, fn_ref, o_ref,
                ubuf, halo, hid_buf, *, tm, fc, dff, final_norm):
    first = pl.program_id(1) == 0

    @pl.when(first)
    def _():
        halo[...] = jnp.zeros_like(halo)

    x = x_ref[...]
    h = _rms(x, g_ref[...]).astype(BF16)
    for f in range(dff // fc):
        lo = f * fc
        ug = _dot(h, wup_ref[:, lo:lo + fc])
        uv = _dot(h, wup_ref[:, dff + lo:dff + lo + fc])
        ubuf[0:SUBLANES, :] = halo[f]
        ubuf[SUBLANES:SUBLANES + tm, :] = ug
        halo[f] = ug[tm - SUBLANES:tm, :]
        gate = (cb_ref[:, lo:lo + fc]
                + cw_ref[2:3, lo:lo + fc] * ug
                + cw_ref[1:2, lo:lo + fc] * ubuf[SUBLANES - 1:SUBLANES - 1 + tm, :]
                + cw_ref[0:1, lo:lo + fc] * ubuf[SUBLANES - 2:SUBLANES - 2 + tm, :])
        hid_buf[:, lo:lo + fc] = (_gelu(gate) * uv).astype(BF16)
    y = x + _dot(hid_buf[...], wdn_ref[...])
    if final_norm:
        y = _rms(y, fn_ref[...])
    o_ref[...] = y


def _ffn(x2d, norm, w_up, conv_w, conv_b, w_down, final_g, *, nb, nt, tm, final_norm):
    m, d = x2d.shape
    dff = w_down.shape[0]
    fc = 256
    row = lambda b, c: (b * nt + c, 0)
    kern = functools.partial(_ffn_kernel, tm=tm, fc=fc, dff=dff, final_norm=final_norm)
    return pl.pallas_call(
        kern,
        grid=(nb, nt),
        in_specs=[
            pl.BlockSpec((tm, d), row),
            _const_spec((1, d)),
            _const_spec((d, 2 * dff)),
            _const_spec((3, dff)),
            _const_spec((1, dff)),
            _const_spec((dff, d)),
            _const_spec((1, d)),
        ],
        out_specs=pl.BlockSpec((tm, d), row),
        out_shape=jax.ShapeDtypeStruct((m, d), F32),
        scratch_shapes=[
            pltpu.VMEM((SUBLANES + tm, fc), F32),
            pltpu.VMEM((dff // fc, SUBLANES, fc), F32),
            pltpu.VMEM((tm, dff), BF16),
        ],
        compiler_params=_params(),
        name="conv_ffn",
    )(x2d, norm.reshape(1, d), w_up.astype(BF16), conv_w, conv_b.reshape(1, dff),
      w_down.astype(BF16), final_g.reshape(1, d))


def _lru_kernel(x_ref, g_ref, win_ref, bin_ref, cw_ref, cb_ref, wg_ref, bg_ref, lam_ref,
                wout_ref, bout_ref, o_ref, xbuf, halo, obuf, hcar, *, tm, d):
    first = pl.program_id(1) == 0

    @pl.when(first)
    def _():
        hcar[...] = jnp.zeros_like(hcar)

    x = x_ref[...]
    h = _rms(x, g_ref[...]).astype(BF16)
    ux = _dot(h, win_ref[:, d:2 * d]) + bin_ref[:, d:2 * d]
    uy = _dot(h, win_ref[:, 0:d]) + bin_ref[:, 0:d]
    _shifted(xbuf, halo, ux, tm, first)
    xr = (cb_ref[...]
          + cw_ref[3:4, :] * ux
          + cw_ref[2:3, :] * xbuf[SUBLANES - 1:SUBLANES - 1 + tm, :]
          + cw_ref[1:2, :] * xbuf[SUBLANES - 2:SUBLANES - 2 + tm, :]
          + cw_ref[0:1, :] * xbuf[SUBLANES - 3:SUBLANES - 3 + tm, :])

    sp = _softplus(-lam_ref[...])
    ngrp = tm // SUBLANES
    sub = lax.broadcasted_iota(jnp.int32, (ngrp, SUBLANES, GATE_TILE), 1)
    for q in range(d // GATE_TILE):
        lo = q * GATE_TILE
        xq = xr[:, lo:lo + GATE_TILE]
        xqb = xq.astype(BF16)
        rg = _sigmoid(_dot(xqb, wg_ref[0, q]) + bg_ref[0:1, lo:lo + GATE_TILE])
        ig = _sigmoid(_dot(xqb, wg_ref[1, q]) + bg_ref[1:2, lo:lo + GATE_TILE])
        log_a = (-LRU_C) * rg * sp[:, lo:lo + GATE_TILE]
        a = jnp.exp(log_a)
        bt = (1.0 + a) * _sqrt_nonneg(-jnp.tanh(0.5 * log_a)) * (ig * xq)
        a3 = a.reshape(ngrp, SUBLANES, GATE_TILE)
        b3 = bt.reshape(ngrp, SUBLANES, GATE_TILE)
        for s in (1, 2, 4):
            keep = sub >= s
            b3 = a3 * jnp.where(keep, pltpu.roll(b3, s, 1), 0.0) + b3
            a3 = a3 * jnp.where(keep, pltpu.roll(a3, s, 1), 1.0)
        hprev = hcar[0:1, lo:lo + GATE_TILE]
        groups = []
        for i in range(ngrp):
            h8 = a3[i] * hprev + b3[i]
            groups.append(h8)
            hprev = h8[SUBLANES - 1:SUBLANES, :]
        hcar[0:1, lo:lo + GATE_TILE] = hprev
        hq = jnp.concatenate(groups, axis=0)
        obuf[:, lo:lo + GATE_TILE] = (hq * _gelu(uy[:, lo:lo + GATE_TILE])).astype(BF16)

    o_ref[...] = x + _dot(obuf[...], wout_ref[...]) + bout_ref[...]


def _lru(x2d, norm, w_in, b_in, conv_w, conv_b, gate_w, gate_b, lam, w_out, b_out, *, nb, nt, tm):
    m, d = x2d.shape
    nblk, bw = gate_w.shape[1], gate_w.shape[2]
    per = GATE_TILE // bw
    w5 = gate_w.reshape(2, nblk // per, per, bw, bw)
    eye = jnp.eye(per, dtype=gate_w.dtype)
    wg = jnp.einsum('gqmcd,mn->gqmcnd', w5, eye).reshape(2, nblk // per, GATE_TILE, GATE_TILE)
    row = lambda b, c: (b * nt + c, 0)
    kern = functools.partial(_lru_kernel, tm=tm, d=d)
    return pl.pallas_call(
        kern,
        grid=(nb, nt),
        in_specs=[
            pl.BlockSpec((tm, d), row),
            _const_spec((1, d)),
            _const_spec((d, 2 * d)),
            _const_spec((1, 2 * d)),
            _const_spec((4, d)),
            _const_spec((1, d)),
            _const_spec((2, d // GATE_TILE, GATE_TILE, GATE_TILE)),
            _const_spec((2, d)),
            _const_spec((1, d)),
            _const_spec((d, d)),
            _const_spec((1, d)),
        ],
        out_specs=pl.BlockSpec((tm, d), row),
        out_shape=jax.ShapeDtypeStruct((m, d), F32),
        scratch_shapes=[
            pltpu.VMEM((SUBLANES + tm, d), F32),
            pltpu.VMEM((SUBLANES, d), F32),
            pltpu.VMEM((tm, d), BF16),
            pltpu.VMEM((SUBLANES, d), F32),
        ],
        compiler_params=_params(),
        name="rglru",
    )(x2d, norm.reshape(1, d), w_in.astype(BF16), b_in.reshape(1, 2 * d), conv_w,
      conv_b.reshape(1, d), wg.astype(BF16), gate_b.reshape(2, d), lam.reshape(1, d),
      w_out.astype(BF16), b_out.reshape(1, d))


def _head_sum(x, jmat):
    hi = x.astype(BF16)
    lo = (x - hi.astype(F32)).astype(BF16)
    return _dot(hi, jmat) + _dot(lo, jmat)


def _pre_kernel(x_ref, g_ref, mix_ref, wrkv_ref, w0_ref, w1_ref, w2_ref, a0_ref, a1_ref, a2_ref,
                g1_ref, g2_ref, kk_ref, ka_ref, rk_ref, jm_ref,
                r_out, c_out, k_out, v_out, kn_out, b_out, g_out, bonus_out,
                hbuf, halo, *, tm, sub, d):
    @pl.when(pl.program_id(1) == 0)
    def _():
        halo[...] = jnp.zeros_like(halo)

    hbuf[0:SUBLANES, :] = halo[...]
    jm = jm_ref[...]
    tin = lax.broadcasted_iota(jnp.int32, (sub, d), 0) & (SCAN_L - 1)
    def front(r0):
        h = _rms(x_ref[r0:r0 + sub, :], g_ref[...])
        hbuf[SUBLANES + r0:SUBLANES + r0 + sub, :] = h
        if r0 + sub == tm:
            halo[...] = h[sub - SUBLANES:sub, :]
        xx = hbuf[SUBLANES - 1 + r0:SUBLANES - 1 + r0 + sub, :] - h

        hb = h.astype(BF16).reshape(sub // BF16_ROWS, BF16_ROWS, d)
        xxb = xx.astype(BF16).reshape(sub // BF16_ROWS, BF16_ROWS, d)

        def mixed(i):
            return (hb + xxb * mix_ref[i]).reshape(sub, d)

        w_lo = _dot(mixed(3), w1_ref[...])
        a_lo = _dot(mixed(4), a1_ref[...])
        g_lo = _dot(mixed(5), g1_ref[...])
        r = _dot(mixed(0), wrkv_ref[0])
        k = _dot(mixed(1), wrkv_ref[1])
        v = _dot(mixed(2), wrkv_ref[2])
        wl = _dot(jnp.tanh(w_lo).astype(BF16), w2_ref[...])
        al = _dot(a_lo.astype(BF16), a2_ref[...])
        g_out[r0:r0 + sub, :] = _dot(_sigmoid(g_lo).astype(BF16), g2_ref[...]).astype(BF16)
        return r0, r, k, v, wl, al

    def tail(piece):
        r0, r, k, v, wl, al = piece
        rows = slice(r0, r0 + sub)
        z = -(w0_ref[...] + wl)
        w = -(jnp.maximum(z, 0.0) + jnp.log(1.0 + jnp.exp(-jnp.abs(z)))) - 0.5
        lw = -jnp.exp(w)
        a = _sigmoid(a0_ref[...] + al)

        c = lw
        s = 1
        while s < SCAN_L:
            c = jnp.where(tin >= s, c + pltpu.roll(c, s, 0), c)
            s *= 2

        kf = k * (1.0 + (a - 1.0) * ka_ref[...])
        kk = k * kk_ref[...]
        rkk = r * kf * rk_ref[...]
        for p in range(d // PAIR):
            sl = slice(p * PAIR, (p + 1) * PAIR)
            kkp = kk[:, sl]
            kn = kkp * jnp.minimum(lax.rsqrt(_head_sum(kkp * kkp, jm)), 1e12)
            kn_out[p, rows, :] = kn
            b_out[p, rows, :] = kn * a[:, sl]
            bonus_out[rows, sl] = (_head_sum(rkk[:, sl], jm) * v[:, sl]).astype(BF16)
            r_out[p, rows, :] = r[:, sl]
            c_out[p, rows, :] = c[:, sl]
            k_out[p, rows, :] = kf[:, sl]
            v_out[p, rows, :] = v[:, sl]

    pending = None
    for r0 in range(0, tm, sub):
        piece = front(r0)
        if pending is not None:
            tail(pending)
        pending = piece
    tail(pending)


def _scan_kernel(r_ref, c_ref, k_ref, v_ref, kn_ref, b_ref, y_ref, s_ref, *, nu):
    L = SCAN_L

    @pl.when(pl.program_id(1) == 0)
    def _():
        s_ref[...] = jnp.zeros_like(s_ref)

    lane = lax.broadcasted_iota(jnp.int32, (L, PAIR), 1)
    trow = lax.broadcasted_iota(jnp.int32, (L, PAIR), 0)
    h0 = lane < HEAD
    col_t = lane & (L - 1)
    incl = col_t <= trow
    srow = lax.broadcasted_iota(jnp.int32, (PAIR, PAIR), 0)
    scol = lax.broadcasted_iota(jnp.int32, (PAIR, PAIR), 1)
    same_head = (srow < HEAD) == (scol < HEAD)
    col2 = lax.broadcasted_iota(jnp.int32, (L, 2 * L), 1)
    trow2 = lax.broadcasted_iota(jnp.int32, (L, 2 * L), 0)
    strict2 = (col2 & (L - 1)) < trow2
    strict = col_t < trow
    nb0 = strict & (lane < L)
    nb1 = strict & (lane >= L) & (lane < 2 * L)
    h0z = lax.broadcasted_iota(jnp.int32, (2 * L, PAIR), 1) < HEAD

    def split(t):
        return jnp.concatenate([jnp.where(h0, t, 0.0), jnp.where(h0, 0.0, t)], axis=0)

    def group(gi, carry):
        units = [gi * SCAN_GROUP + j for j in range(SCAN_GROUP)]
        lhs, rhs, vm, vs, bk, gl = [], [], [], [], [], []
        for u in units:
            r = r_ref[u]
            c = c_ref[u]
            k = k_ref[u]
            v = v_ref[u]
            kn = kn_ref[u]
            bv = b_ref[u]
            c_last = c[L - 1:L, :]
            c_excl = jnp.where(trow >= 1, pltpu.roll(c, 1, 0), 0.0)
            e_neg = jnp.exp(-c)
            e_end = jnp.exp(c_last - c)
            rt = r * jnp.exp(c)
            at = -(kn * jnp.exp(c_excl))
            lhs.append(jnp.concatenate([at, rt], axis=0).astype(BF16))
            rhs.append(jnp.concatenate([split(bv * e_neg), split(k * e_neg)], axis=0).astype(BF16))
            vm.append(split(v))
            vs.append(v)
            bk.append(jnp.concatenate([bv * e_end, k * e_end], axis=0).astype(BF16))
            gl.append(jnp.exp(c_last))
        st = [s_ref[u] for u in units]
        ps = [_dot_nt(a, jnp.concatenate([b, s.astype(BF16)], axis=0)) for a, b, s in zip(lhs, rhs, st)]
        pbot = [jnp.where(incl, q[L:2 * L, 0:PAIR], 0.0) for q in ps]
        wak = [_dot(jnp.where(strict2, q[0:L, 2 * L:4 * L], 0.0).astype(BF16), m.astype(BF16))
               for q, m in zip(ps, vm)]
        w = [q[0:L, PAIR:2 * PAIR] + a for q, a in zip(ps, wak)]
        z = [jnp.concatenate(
            [jnp.where(h0, jnp.where(nb0, q[0:L, 0:PAIR], 0.0), pltpu.roll(ww, HEAD, 1)),
             jnp.where(h0, jnp.where(nb1, q[0:L, 0:PAIR], 0.0), ww)], axis=0)
             for q, ww in zip(ps, w)]
        for _ in range(SOLVE_ROUNDS):
            out = [_dot(zz[:, 0:2 * L].astype(BF16), zz.astype(BF16)) for zz in z]
            z = [o + jnp.where(h0z, 0.0, zz) for o, zz in zip(out, z)]
        uu = [jnp.where(h0, pltpu.roll(zz[0:L], HEAD, 1), zz[L:2 * L]) for zz in z]
        for u, q, pb, xx, m in zip(units, ps, pbot, uu, vm):
            y_ref[u] = q[L:2 * L, PAIR:2 * PAIR] + _dot(
                pb.astype(BF16), jnp.concatenate([split(xx), m], axis=0).astype(BF16))
        for u, s, xx, v, b, g in zip(units, st, uu, vs, bk, gl):
            uv_t = jnp.concatenate([xx, v], axis=0).T.astype(BF16)
            s_ref[u] = jnp.where(same_head, s * g + _dot(uv_t, b), 0.0)
        return carry

    lax.fori_loop(0, nu // SCAN_GROUP, group, 0)


def _post_kernel(y_ref, bonus_ref, g_ref, x_ref, lnw_ref, lnb_ref, wout_ref, jm_ref, o_ref,
                 zbuf, *, d):
    jm = jm_ref[...]
    for p in range(d // PAIR):
        sl = slice(p * PAIR, (p + 1) * PAIR)
        y = y_ref[p]
        mu = _head_sum(y, jm) * (1.0 / HEAD)
        dy = y - mu
        var = _head_sum(dy * dy, jm) * (1.0 / HEAD)
        yn = dy * lax.rsqrt(var + GN_EPS) * lnw_ref[:, sl] + lnb_ref[:, sl]
        zbuf[:, sl] = ((yn + bonus_ref[:, sl].astype(F32)) * g_ref[:, sl].astype(F32)).astype(BF16)
    o_ref[...] = x_ref[...] + _dot(zbuf[...], wout_ref[...])


def _rwkv(x2d, norm, mix, w_rkv, w0, w1, w2, a0, a1, a2, g1, g2, k_k, k_a, r_k, ln_w, ln_b, w_out,
          *, nb, nt_pre, tm_pre, nt_post, tm_post, seq):
    m, d = x2d.shape
    npair = d // PAIR
    assert 4 * SCAN_L == PAIR and tm_pre % SCAN_L == 0
    idx = jnp.arange(PAIR) // HEAD
    jm = (idx[:, None] == idx[None, :]).astype(BF16)
    vec = lambda t: t.reshape(1, d)
    row_pre = lambda b, c: (b * nt_pre + c, 0)
    pair_pre = lambda b, c: (b, c, 0)
    pair_shape = jax.ShapeDtypeStruct((nb * npair, seq, PAIR), F32)
    pair_spec = pl.BlockSpec((npair, tm_pre, PAIR), pair_pre)
    flat_shape = jax.ShapeDtypeStruct((m, d), F32)
    r, c, k, v, kn, bv, g, bonus = pl.pallas_call(
        functools.partial(_pre_kernel, tm=tm_pre, sub=min(tm_pre, PRE_SUB), d=d),
        grid=(nb, nt_pre),
        in_specs=[
            pl.BlockSpec((tm_pre, d), row_pre),
            _const_spec((1, d)),
            _const_spec((6, BF16_ROWS, d)),
            _const_spec((3, d, d)),
            _const_spec((1, d)),
            _const_spec(w1.shape),
            _const_spec(w2.shape),
            _const_spec((1, d)),
            _const_spec(a1.shape),
            _const_spec(a2.shape),
            _const_spec(g1.shape),
            _const_spec(g2.shape),
            _const_spec((1, d)),
            _const_spec((1, d)),
            _const_spec((1, d)),
            _const_spec((PAIR, PAIR)),
        ],
        out_specs=[pair_spec] * 6 + [pl.BlockSpec((tm_pre, d), row_pre)] * 2,
        out_shape=[pair_shape] * 6 + [jax.ShapeDtypeStruct((m, d), BF16)] * 2,
        scratch_shapes=[
            pltpu.VMEM((SUBLANES + tm_pre, d), F32),
            pltpu.VMEM((SUBLANES, d), F32),
        ],
        compiler_params=_params(),
        name="rwkv_pre",
    )(x2d, vec(norm), jnp.broadcast_to(mix[:, None, :], (6, BF16_ROWS, d)).astype(BF16),
      w_rkv.astype(BF16), vec(w0), w1.astype(BF16), w2.astype(BF16),
      vec(a0), a1.astype(BF16), a2.astype(BF16), g1.astype(BF16), g2.astype(BF16),
      vec(k_k), vec(k_a), vec(r_k), jm)

    nu = 4 * npair
    unit_spec = pl.BlockSpec((nu, SCAN_L, PAIR), lambda ug, c: (ug, c, 0))
    y = pl.pallas_call(
        functools.partial(_scan_kernel, nu=nu),
        grid=(nb * npair // nu, seq // SCAN_L),
        in_specs=[unit_spec] * 6,
        out_specs=unit_spec,
        out_shape=pair_shape,
        scratch_shapes=[pltpu.VMEM((nu, PAIR, PAIR), F32)],
        compiler_params=_params(),
        name="rwkv_scan",
    )(r, c, k, v, kn, bv)

    row_post = lambda b, c: (b * nt_post + c, 0)
    return pl.pallas_call(
        functools.partial(_post_kernel, d=d),
        grid=(nb, nt_post),
        in_specs=[
            pl.BlockSpec((npair, tm_post, PAIR), lambda b, c: (b, c, 0)),
            pl.BlockSpec((tm_post, d), row_post),
            pl.BlockSpec((tm_post, d), row_post),
            pl.BlockSpec((tm_post, d), row_post),
            _const_spec((1, d)),
            _const_spec((1, d)),
            _const_spec((d, d)),
            _const_spec((PAIR, PAIR)),
        ],
        out_specs=pl.BlockSpec((tm_post, d), row_post),
        out_shape=flat_shape,
        scratch_shapes=[pltpu.VMEM((tm_post, d), BF16)],
        compiler_params=_params(),
        name="rwkv_post",
    )(y, bonus, g, x2d, vec(ln_w), vec(ln_b), w_out.astype(BF16), jm)


def _tile(seq, want):
    return want if seq % want == 0 else seq


def kernel(x, lru_norm, lru_w_in, lru_b_in, lru_conv_w, lru_conv_b, lru_gate_w, lru_gate_b, lru_lambda, lru_w_out, lru_b_out, rwkv_norm, rwkv_mix, rwkv_w_rkv, rwkv_w0, rwkv_w1, rwkv_w2, rwkv_a0, rwkv_a1, rwkv_a2, rwkv_g1, rwkv_g2, rwkv_k_k, rwkv_k_a, rwkv_r_k, rwkv_ln_w, rwkv_ln_b, rwkv_w_out, ffn_norm, ffn_w_up, ffn_conv_w, ffn_conv_b, ffn_w_down, final_norm):
    nb, seq, d = x.shape
    depth = ffn_norm.shape[0]
    tm = _tile(seq, 512)
    tm_pre = _tile(seq, 512)
    tm_wide = _tile(seq, 1024)
    nt = seq // tm
    h = x.reshape(nb * seq, d)
    for layer in range(depth):
        j = layer // 2
        if layer % 2 == 0:
            h = _lru(h, lru_norm[j], lru_w_in[j], lru_b_in[j], lru_conv_w[j], lru_conv_b[j],
                     lru_gate_w[j], lru_gate_b[j], lru_lambda[j], lru_w_out[j], lru_b_out[j],
                     nb=nb, nt=seq // tm_wide, tm=tm_wide)
        else:
            h = _rwkv(h, rwkv_norm[j], rwkv_mix[j], rwkv_w_rkv[j], rwkv_w0[j], rwkv_w1[j],
                      rwkv_w2[j], rwkv_a0[j], rwkv_a1[j], rwkv_a2[j], rwkv_g1[j], rwkv_g2[j],
                      rwkv_k_k[j], rwkv_k_a[j], rwkv_r_k[j], rwkv_ln_w[j], rwkv_ln_b[j],
                      rwkv_w_out[j], nb=nb, nt_pre=seq // tm_pre, tm_pre=tm_pre,
                      nt_post=nt, tm_post=tm, seq=seq)
        h = _ffn(h, ffn_norm[layer], ffn_w_up[layer], ffn_conv_w[layer], ffn_conv_b[layer],
                 ffn_w_down[layer], final_norm, nb=nb, nt=seq // tm_wide, tm=tm_wide,
                 final_norm=(layer == depth - 1))
    return h.reshape(nb, seq, d)
```

```python
import functools

import jax
import jax.numpy as jnp
from jax import lax
from jax.experimental import pallas as pl
from jax.experimental.pallas import tpu as pltpu

F32 = jnp.float32
BF16 = jnp.bfloat16

RMS_EPS = 1e-6
GN_EPS = 64e-5
LRU_C = 8.0
HEAD = 64
PAIR = 2 * HEAD
GATE_TILE = 256
SCAN_L = 32
PRE_SUB = 128
SOLVE_ROUNDS = 5
SUBLANES = 8
LANES = 128
BF16_ROWS = 16
SCAN_GROUP = 32

VMEM_LIMIT = 56 * 1024 * 1024


def _rms(x, g):
    ms = jnp.mean(x * x, axis=-1, keepdims=True)
    return x * lax.rsqrt(ms + RMS_EPS) * g


def _gelu(x):
    return x * (0.5 * (1.0 + jnp.tanh(0.7978845608028654 * (x + 0.044715 * (x * x * x)))))


def _sigmoid(x):
    return 0.5 * jnp.tanh(0.5 * x) + 0.5


def _sqrt_nonneg(x):
    return jnp.where(x > 0.0, x * lax.rsqrt(x), 0.0)


def _softplus(x):
    return jnp.maximum(x, 0.0) + jnp.log1p(jnp.exp(-jnp.abs(x)))


def _dot(a, b):
    return jnp.dot(a, b, preferred_element_type=F32)


def _dot_nt(a, b):
    return lax.dot_general(a, b, (((1,), (1,)), ((), ())), preferred_element_type=F32)


def _const_spec(shape):
    nd = len(shape)
    return pl.BlockSpec(shape, lambda *_: (0,) * nd, pipeline_mode=pl.Buffered(1))


def _params():
    return pltpu.CompilerParams(dimension_semantics=("arbitrary", "arbitrary"),
                                vmem_limit_bytes=VMEM_LIMIT)


def _shifted(buf, halo, cur, tm, first):
    @pl.when(first)
    def _():
        halo[...] = jnp.zeros_like(halo)
    buf[0:SUBLANES, :] = halo[...]
    buf[SUBLANES:SUBLANES + tm, :] = cur
    halo[...] = cur[tm - SUBLANES:tm, :]


def _ffn_kernel(x_ref, g_ref, wup_ref, cw_ref, cb_ref, wdn_ref, fn_ref, o_ref,
                ubuf, halo, hid_buf, *, tm, fc, dff, final_norm):
    first = pl.program_id(1) == 0

    @pl.when(first)
    def _():
        halo[...] = jnp.zeros_like(halo)

    x = x_ref[...]
    h = _rms(x, g_ref[...]).astype(BF16)
    for f in range(dff // fc):
        lo = f * fc
        ug = _dot(h, wup_ref[:, lo:lo + fc])
        uv = _dot(h, wup_ref[:, dff + lo:dff + lo + fc])
        ubuf[0:SUBLANES, :] = halo[f]
        ubuf[SUBLANES:SUBLANES + tm, :] = ug
        halo[f] = ug[tm - SUBLANES:tm, :]
        gate = (cb_ref[:, lo:lo + fc]
                + cw_ref[2:3, lo:lo + fc] * ug
                + cw_ref[1:2, lo:lo + fc] * ubuf[SUBLANES - 1:SUBLANES - 1 + tm, :]
                + cw_ref[0:1, lo:lo + fc] * ubuf[SUBLANES - 2:SUBLANES - 2 + tm, :])
        hid_buf[:, lo:lo + fc] = (_gelu(gate) * uv).astype(BF16)
    y = x + _dot(hid_buf[...], wdn_ref[...])
    if final_norm:
        y = _rms(y, fn_ref[...])
    o_ref[...] = y


def _ffn(x2d, norm, w_up, conv_w, conv_b, w_down, final_g, *, nb, nt, tm, final_norm):
    m, d = x2d.shape
    dff = w_down.shape[0]
    fc = 256
    row = lambda b, c: (b * nt + c, 0)
    kern = functools.partial(_ffn_kernel, tm=tm, fc=fc, dff=dff, final_norm=final_norm)
    return pl.pallas_call(
        kern,
        grid=(nb, nt),
        in_specs=[
            pl.BlockSpec((tm, d), row),
            _const_spec((1, d)),
            _const_spec((d, 2 * dff)),
            _const_spec((3, dff)),
            _const_spec((1, dff)),
            _const_spec((dff, d)),
            _const_spec((1, d)),
        ],
        out_specs=pl.BlockSpec((tm, d), row),
        out_shape=jax.ShapeDtypeStruct((m, d), F32),
        scratch_shapes=[
            pltpu.VMEM((SUBLANES + tm, fc), F32),
            pltpu.VMEM((dff // fc, SUBLANES, fc), F32),
            pltpu.VMEM((tm, dff), BF16),
        ],
        compiler_params=_params(),
        name="conv_ffn",
    )(x2d, norm.reshape(1, d), w_up.astype(BF16), conv_w, conv_b.reshape(1, dff),
      w_down.astype(BF16), final_g.reshape(1, d))


def _lru_kernel(x_ref, g_ref, win_ref, bin_ref, cw_ref, cb_ref, wg_ref, bg_ref, lam_ref,
                wout_ref, bout_ref, o_ref, xbuf, halo, obuf, hcar, *, tm, d):
    first = pl.program_id(1) == 0

    @pl.when(first)
    def _():
        hcar[...] = jnp.zeros_like(hcar)

    x = x_ref[...]
    h = _rms(x, g_ref[...]).astype(BF16)
    ux = _dot(h, win_ref[:, d:2 * d]) + bin_ref[:, d:2 * d]
    uy = _dot(h, win_ref[:, 0:d]) + bin_ref[:, 0:d]
    _shifted(xbuf, halo, ux, tm, first)
    xr = (cb_ref[...]
          + cw_ref[3:4, :] * ux
          + cw_ref[2:3, :] * xbuf[SUBLANES - 1:SUBLANES - 1 + tm, :]
          + cw_ref[1:2, :] * xbuf[SUBLANES - 2:SUBLANES - 2 + tm, :]
          + cw_ref[0:1, :] * xbuf[SUBLANES - 3:SUBLANES - 3 + tm, :])

    sp = _softplus(-lam_ref[...])
    ngrp = tm // SUBLANES
    sub = lax.broadcasted_iota(jnp.int32, (ngrp, SUBLANES, GATE_TILE), 1)
    for q in range(d // GATE_TILE):
        lo = q * GATE_TILE
        xq = xr[:, lo:lo + GATE_TILE]
        xqb = xq.astype(BF16)
        rg = _sigmoid(_dot(xqb, wg_ref[0, q]) + bg_ref[0:1, lo:lo + GATE_TILE])
        ig = _sigmoid(_dot(xqb, wg_ref[1, q]) + bg_ref[1:2, lo:lo + GATE_TILE])
        log_a = (-LRU_C) * rg * sp[:, lo:lo + GATE_TILE]
        a = jnp.exp(log_a)
        bt = (1.0 + a) * _sqrt_nonneg(-jnp.tanh(0.5 * log_a)) * (ig * xq)
        a3 = a.reshape(ngrp, SUBLANES, GATE_TILE)
        b3 = bt.reshape(ngrp, SUBLANES, GATE_TILE)
        for s in (1, 2, 4):
            keep = sub >= s
            b3 = a3 * jnp.where(keep, pltpu.roll(b3, s, 1), 0.0) + b3
            a3 = a3 * jnp.where(keep, pltpu.roll(a3, s, 1), 1.0)
        hprev = hcar[0:1, lo:lo + GATE_TILE]
        groups = []
        for i in range(ngrp):
            h8 = a3[i] * hprev + b3[i]
            groups.append(h8)
            hprev = h8[SUBLANES - 1:SUBLANES, :]
        hcar[0:1, lo:lo + GATE_TILE] = hprev
        hq = jnp.concatenate(groups, axis=0)
        obuf[:, lo:lo + GATE_TILE] = (hq * _gelu(uy[:, lo:lo + GATE_TILE])).astype(BF16)

    o_ref[...] = x + _dot(obuf[...], wout_ref[...]) + bout_ref[...]


def _lru(x2d, norm, w_in, b_in, conv_w, conv_b, gate_w, gate_b, lam, w_out, b_out, *, nb, nt, tm):
    m, d = x2d.shape
    nblk, bw = gate_w.shape[1], gate_w.shape[2]
    per = GATE_TILE // bw
    w5 = gate_w.reshape(2, nblk // per, per, bw, bw)
    eye = jnp.eye(per, dtype=gate_w.dtype)
    wg = jnp.einsum('gqmcd,mn->gqmcnd', w5, eye).reshape(2, nblk // per, GATE_TILE, GATE_TILE)
    row = lambda b, c: (b * nt + c, 0)
    kern = functools.partial(_lru_kernel, tm=tm, d=d)
    return pl.pallas_call(
        kern,
        grid=(nb, nt),
        in_specs=[
            pl.BlockSpec((tm, d), row),
            _const_spec((1, d)),
            _const_spec((d, 2 * d)),
            _const_spec((1, 2 * d)),
            _const_spec((4, d)),
            _const_spec((1, d)),
            _const_spec((2, d // GATE_TILE, GATE_TILE, GATE_TILE)),
            _const_spec((2, d)),
            _const_spec((1, d)),
            _const_spec((d, d)),
            _const_spec((1, d)),
        ],
        out_specs=pl.BlockSpec((tm, d), row),
        out_shape=jax.ShapeDtypeStruct((m, d), F32),
        scratch_shapes=[
            pltpu.VMEM((SUBLANES + tm, d), F32),
            pltpu.VMEM((SUBLANES, d), F32),
            pltpu.VMEM((tm, d), BF16),
            pltpu.VMEM((SUBLANES, d), F32),
        ],
        compiler_params=_params(),
        name="rglru",
    )(x2d, norm.reshape(1, d), w_in.astype(BF16), b_in.reshape(1, 2 * d), conv_w,
      conv_b.reshape(1, d), wg.astype(BF16), gate_b.reshape(2, d), lam.reshape(1, d),
      w_out.astype(BF16), b_out.reshape(1, d))


def _head_sum(x, jmat):
    hi = x.astype(BF16)
    lo = (x - hi.astype(F32)).astype(BF16)
    return _dot(hi, jmat) + _dot(lo, jmat)


def _pre_kernel(x_ref, g_ref, mix_ref, wrkv_ref, w0_ref, w1_ref, w2_ref, a0_ref, a1_ref, a2_ref,
                g1_ref, g2_ref, kk_ref, ka_ref, rk_ref, jm_ref,
                r_out, c_out, k_out, v_out, kn_out, b_out, g_out, bonus_out,
                hbuf, halo, *, tm, sub, d):
    @pl.when(pl.program_id(1) == 0)
    def _():
        halo[...] = jnp.zeros_like(halo)

    hbuf[0:SUBLANES, :] = halo[...]
    jm = jm_ref[...]
    tin = lax.broadcasted_iota(jnp.int32, (sub, d), 0) & (SCAN_L - 1)
    def front(r0):
        h = _rms(x_ref[r0:r0 + sub, :], g_ref[...])
        hbuf[SUBLANES + r0:SUBLANES + r0 + sub, :] = h
        if r0 + sub == tm:
            halo[...] = h[sub - SUBLANES:sub, :]
        xx = hbuf[SUBLANES - 1 + r0:SUBLANES - 1 + r0 + sub, :] - h

        hb = h.astype(BF16).reshape(sub // BF16_ROWS, BF16_ROWS, d)
        xxb = xx.astype(BF16).reshape(sub // BF16_ROWS, BF16_ROWS, d)

        def mixed(i):
            return (hb + xxb * mix_ref[i]).reshape(sub, d)

        w_lo = _dot(mixed(3), w1_ref[...])
        a_lo = _dot(mixed(4), a1_ref[...])
        g_lo = _dot(mixed(5), g1_ref[...])
        r = _dot(mixed(0), wrkv_ref[0])
        k = _dot(mixed(1), wrkv_ref[1])
        v = _dot(mixed(2), wrkv_ref[2])
        wl = _dot(jnp.tanh(w_lo).astype(BF16), w2_ref[...])
        al = _dot(a_lo.astype(BF16), a2_ref[...])
        g_out[r0:r0 + sub, :] = _dot(_sigmoid(g_lo).astype(BF16), g2_ref[...]).astype(BF16)
        return r0, r, k, v, wl, al

    def tail(piece):
        r0, r, k, v, wl, al = piece
        rows = slice(r0, r0 + sub)
        z = -(w0_ref[...] + wl)
        w = -(jnp.maximum(z, 0.0) + jnp.log(1.0 + jnp.exp(-jnp.abs(z)))) - 0.5
        lw = -jnp.exp(w)
        a = _sigmoid(a0_ref[...] + al)

        c = lw
        s = 1
        while s < SCAN_L:
            c = jnp.where(tin >= s, c + pltpu.roll(c, s, 0), c)
            s *= 2

        kf = k * (1.0 + (a - 1.0) * ka_ref[...])
        kk = k * kk_ref[...]
        rkk = r * kf * rk_ref[...]
        for p in range(d // PAIR):
            sl = slice(p * PAIR, (p + 1) * PAIR)
            kkp = kk[:, sl]
            kn = kkp * jnp.minimum(lax.rsqrt(_head_sum(kkp * kkp, jm)), 1e12)
            kn_out[p, rows, :] = kn
            b_out[p, rows, :] = kn * a[:, sl]
            bonus_out[rows, sl] = (_head_sum(rkk[:, sl], jm) * v[:, sl]).astype(BF16)
            r_out[p, rows, :] = r[:, sl]
            c_out[p, rows, :] = c[:, sl]
            k_out[p, rows, :] = kf[:, sl]
            v_out[p, rows, :] = v[:, sl]

    pending = None
    for r0 in range(0, tm, sub):
        piece = front(r0)
        if pending is not None:
            tail(pending)
        pending = piece
    tail(pending)


def _scan_kernel(r_ref, c_ref, k_ref, v_ref, kn_ref, b_ref, y_ref, s_ref, *, nu):
    L = SCAN_L

    @pl.when(pl.program_id(1) == 0)
    def _():
        s_ref[...] = jnp.zeros_like(s_ref)

    lane = lax.broadcasted_iota(jnp.int32, (L, PAIR), 1)
    trow = lax.broadcasted_iota(jnp.int32, (L, PAIR), 0)
    h0 = lane < HEAD
    col_t = lane & (L - 1)
    incl = col_t <= trow
    srow = lax.broadcasted_iota(jnp.int32, (PAIR, PAIR), 0)
    scol = lax.broadcasted_iota(jnp.int32, (PAIR, PAIR), 1)
    same_head = (srow < HEAD) == (scol < HEAD)
    col2 = lax.broadcasted_iota(jnp.int32, (L, 2 * L), 1)
    trow2 = lax.broadcasted_iota(jnp.int32, (L, 2 * L), 0)
    strict2 = (col2 & (L - 1)) < trow2
    strict = col_t < trow
    nb0 = strict & (lane < L)
    nb1 = strict & (lane >= L) & (lane < 2 * L)
    h0z = lax.broadcasted_iota(jnp.int32, (2 * L, PAIR), 1) < HEAD

    def split(t):
        return jnp.concatenate([jnp.where(h0, t, 0.0), jnp.where(h0, 0.0, t)], axis=0)

    def group(gi, carry):
        units = [gi * SCAN_GROUP + j for j in range(SCAN_GROUP)]
        lhs, rhs, vm, vs, bk, gl = [], [], [], [], [], []
        for u in units:
            r = r_ref[u]
            c = c_ref[u]
            k = k_ref[u]
            v = v_ref[u]
            kn = kn_ref[u]
            bv = b_ref[u]
            c_last = c[L - 1:L, :]
            c_excl = jnp.where(trow >= 1, pltpu.roll(c, 1, 0), 0.0)
            e_neg = jnp.exp(-c)
            e_end = jnp.exp(c_last - c)
            rt = r * jnp.exp(c)
            at = -(kn * jnp.exp(c_excl))
            lhs.append(jnp.concatenate([at, rt], axis=0).astype(BF16))
            rhs.append(jnp.concatenate([split(bv * e_neg), split(k * e_neg)], axis=0).astype(BF16))
            vm.append(split(v))
            vs.append(v)
            bk.append(jnp.concatenate([bv * e_end, k * e_end], axis=0).astype(BF16))
            gl.append(jnp.exp(c_last))
        st = [s_ref[u] for u in units]
        ps = [_dot_nt(a, jnp.concatenate([b, s.astype(BF16)], axis=0)) for a, b, s in zip(lhs, rhs, st)]
        pbot = [jnp.where(incl, q[L:2 * L, 0:PAIR], 0.0) for q in ps]
        wak = [_dot(jnp.where(strict2, q[0:L, 2 * L:4 * L], 0.0).astype(BF16), m.astype(BF16))
               for q, m in zip(ps, vm)]
        w = [q[0:L, PAIR:2 * PAIR] + a for q, a in zip(ps, wak)]
        z = [jnp.concatenate(
            [jnp.where(h0, jnp.where(nb0, q[0:L, 0:PAIR], 0.0), pltpu.roll(ww, HEAD, 1)),
             jnp.where(h0, jnp.where(nb1, q[0:L, 0:PAIR], 0.0), ww)], axis=0)
             for q, ww in zip(ps, w)]
        for _ in range(SOLVE_ROUNDS):
            out = [_dot(zz[:, 0:2 * L].astype(BF16), zz.astype(BF16)) for zz in z]
            z = [o + jnp.where(h0z, 0.0, zz) for o, zz in zip(out, z)]
        uu = [jnp.where(h0, pltpu.roll(zz[0:L], HEAD, 1), zz[L:2 * L]) for zz in z]
        for u, q, pb, xx, m in zip(units, ps, pbot, uu, vm):
            y_ref[u] = (q[L:2 * L, PAIR:2 * PAIR] + _dot(
                pb.astype(BF16), jnp.concatenate([split(xx), m], axis=0).astype(BF16))).astype(BF16)
        for u, s, xx, v, b, g in zip(units, st, uu, vs, bk, gl):
            uv_t = jnp.concatenate([xx, v], axis=0).T.astype(BF16)
            s_ref[u] = jnp.where(same_head, s * g + _dot(uv_t, b), 0.0)
        return carry

    lax.fori_loop(0, nu // SCAN_GROUP, group, 0)


def _post_kernel(y_ref, bonus_ref, g_ref, x_ref, lnw_ref, lnb_ref, wout_ref, jm_ref, o_ref,
                 zbuf, *, d):
    jm = jm_ref[...]
    for p in range(d // PAIR):
        sl = slice(p * PAIR, (p + 1) * PAIR)
        y = y_ref[p].astype(F32)
        mu = _head_sum(y, jm) * (1.0 / HEAD)
        dy = y - mu
        var = _head_sum(dy * dy, jm) * (1.0 / HEAD)
        yn = dy * lax.rsqrt(var + GN_EPS) * lnw_ref[:, sl] + lnb_ref[:, sl]
        zbuf[:, sl] = ((yn + bonus_ref[:, sl].astype(F32)) * g_ref[:, sl].astype(F32)).astype(BF16)
    o_ref[...] = x_ref[...] + _dot(zbuf[...], wout_ref[...])


def _rwkv(x2d, norm, mix, w_rkv, w0, w1, w2, a0, a1, a2, g1, g2, k_k, k_a, r_k, ln_w, ln_b, w_out,
          *, nb, nt_pre, tm_pre, nt_post, tm_post, seq):
    m, d = x2d.shape
    npair = d // PAIR
    assert 4 * SCAN_L == PAIR and tm_pre % SCAN_L == 0
    idx = jnp.arange(PAIR) // HEAD
    jm = (idx[:, None] == idx[None, :]).astype(BF16)
    vec = lambda t: t.reshape(1, d)
    row_pre = lambda b, c: (b * nt_pre + c, 0)
    pair_pre = lambda b, c: (b, c, 0)
    pair_shape = jax.ShapeDtypeStruct((nb * npair, seq, PAIR), F32)
    pair_spec = pl.BlockSpec((npair, tm_pre, PAIR), pair_pre)
    flat_shape = jax.ShapeDtypeStruct((m, d), F32)
    r, c, k, v, kn, bv, g, bonus = pl.pallas_call(
        functools.partial(_pre_kernel, tm=tm_pre, sub=min(tm_pre, PRE_SUB), d=d),
        grid=(nb, nt_pre),
        in_specs=[
            pl.BlockSpec((tm_pre, d), row_pre),
            _const_spec((1, d)),
            _const_spec((6, BF16_ROWS, d)),
            _const_spec((3, d, d)),
            _const_spec((1, d)),
            _const_spec(w1.shape),
            _const_spec(w2.shape),
            _const_spec((1, d)),
            _const_spec(a1.shape),
            _const_spec(a2.shape),
            _const_spec(g1.shape),
            _const_spec(g2.shape),
            _const_spec((1, d)),
            _const_spec((1, d)),
            _const_spec((1, d)),
            _const_spec((PAIR, PAIR)),
        ],
        out_specs=[pair_spec] * 6 + [pl.BlockSpec((tm_pre, d), row_pre)] * 2,
        out_shape=[pair_shape] * 6 + [jax.ShapeDtypeStruct((m, d), BF16)] * 2,
        scratch_shapes=[
            pltpu.VMEM((SUBLANES + tm_pre, d), F32),
            pltpu.VMEM((SUBLANES, d), F32),
        ],
        compiler_params=_params(),
        name="rwkv_pre",
    )(x2d, vec(norm), jnp.broadcast_to(mix[:, None, :], (6, BF16_ROWS, d)).astype(BF16),
      w_rkv.astype(BF16), vec(w0), w1.astype(BF16), w2.astype(BF16),
      vec(a0), a1.astype(BF16), a2.astype(BF16), g1.astype(BF16), g2.astype(BF16),
      vec(k_k), vec(k_a), vec(r_k), jm)

    nu = 4 * npair
    unit_spec = pl.BlockSpec((nu, SCAN_L, PAIR), lambda ug, c: (ug, c, 0))
    y = pl.pallas_call(
        functools.partial(_scan_kernel, nu=nu),
        grid=(nb * npair // nu, seq // SCAN_L),
        in_specs=[unit_spec] * 6,
        out_specs=unit_spec,
        out_shape=jax.ShapeDtypeStruct(pair_shape.shape, BF16),
        scratch_shapes=[pltpu.VMEM((nu, PAIR, PAIR), F32)],
        compiler_params=_params(),
        name="rwkv_scan",
    )(r, c, k, v, kn, bv)

    row_post = lambda b, c: (b * nt_post + c, 0)
    return pl.pallas_call(
        functools.partial(_post_kernel, d=d),
        grid=(nb, nt_post),
        in_specs=[
            pl.BlockSpec((npair, tm_post, PAIR), lambda b, c: (b, c, 0)),
            pl.BlockSpec((tm_post, d), row_post),
            pl.BlockSpec((tm_post, d), row_post),
            pl.BlockSpec((tm_post, d), row_post),
            _const_spec((1, d)),
            _const_spec((1, d)),
            _const_spec((d, d)),
            _const_spec((PAIR, PAIR)),
        ],
        out_specs=pl.BlockSpec((tm_post, d), row_post),
        out_shape=flat_shape,
        scratch_shapes=[pltpu.VMEM((tm_post, d), BF16)],
        compiler_params=_params(),
        name="rwkv_post",
    )(y, bonus, g, x2d, vec(ln_w), vec(ln_b), w_out.astype(BF16), jm)


def _tile(seq, want):
    return want if seq % want == 0 else seq


def kernel(x, lru_norm, lru_w_in, lru_b_in, lru_conv_w, lru_conv_b, lru_gate_w, lru_gate_b, lru_lambda, lru_w_out, lru_b_out, rwkv_norm, rwkv_mix, rwkv_w_rkv, rwkv_w0, rwkv_w1, rwkv_w2, rwkv_a0, rwkv_a1, rwkv_a2, rwkv_g1, rwkv_g2, rwkv_k_k, rwkv_k_a, rwkv_r_k, rwkv_ln_w, rwkv_ln_b, rwkv_w_out, ffn_norm, ffn_w_up, ffn_conv_w, ffn_conv_b, ffn_w_down, final_norm):
    nb, seq, d = x.shape
    depth = ffn_norm.shape[0]
    tm_pre = _tile(seq, 512)
    tm_wide = _tile(seq, 1024)
    h = x.reshape(nb * seq, d)
    for layer in range(depth):
        j = layer // 2
        if layer % 2 == 0:
            h = _lru(h, lru_norm[j], lru_w_in[j], lru_b_in[j], lru_conv_w[j], lru_conv_b[j],
                     lru_gate_w[j], lru_gate_b[j], lru_lambda[j], lru_w_out[j], lru_b_out[j],
                     nb=nb, nt=seq // tm_wide, tm=tm_wide)
        else:
            h = _rwkv(h, rwkv_norm[j], rwkv_mix[j], rwkv_w_rkv[j], rwkv_w0[j], rwkv_w1[j],
                      rwkv_w2[j], rwkv_a0[j], rwkv_a1[j], rwkv_a2[j], rwkv_g1[j], rwkv_g2[j],
                      rwkv_k_k[j], rwkv_k_a[j], rwkv_r_k[j], rwkv_ln_w[j], rwkv_ln_b[j],
                      rwkv_w_out[j], nb=nb, nt_pre=seq // tm_pre, tm_pre=tm_pre,
                      nt_post=seq // tm_wide, tm_post=tm_wide, seq=seq)
        h = _ffn(h, ffn_norm[layer], ffn_w_up[layer], ffn_conv_w[layer], ffn_conv_b[layer],
                 ffn_w_down[layer], final_norm, nb=nb, nt=seq // tm_wide, tm=tm_wide,
                 final_norm=(layer == depth - 1))
    return h.reshape(nb, seq, d)
```
